```python
import math
import jax, jax.numpy as jnp
from jax import lax
import numpy as np

D_MODEL = 1024
BATCH = 16
SEQ = 2048
DEPTH = 1

SB_HEADS = 16
SB_HEAD_DIM = 64
SB_WIDTH = SB_HEADS * SB_HEAD_DIM
Q_BLOCK = 128
SSD_EXPAND = 2
SSD_WIDTH = SSD_EXPAND * D_MODEL
SSD_HEAD_DIM = 64
SSD_HEADS = SSD_WIDTH // SSD_HEAD_DIM
SSD_GROUPS = 4
SSD_HEADS_PER_GROUP = SSD_HEADS // SSD_GROUPS
SSD_STATE = 128
SSD_CONV = 4
SSD_CHUNK = 128
SSD_BC_WIDTH = SSD_GROUPS * SSD_STATE
SSD_CONV_DIM = SSD_WIDTH + 2 * SSD_BC_WIDTH
N_BRANCHES = 2
PROJ_SPLITS = (SB_WIDTH, SB_WIDTH, SB_WIDTH, SB_WIDTH, SSD_WIDTH, SSD_CONV_DIM, SSD_HEADS, N_BRANCHES * D_MODEL)
D_PROJ = 4 * SB_WIDTH + SSD_WIDTH + SSD_CONV_DIM + SSD_HEADS + N_BRANCHES * D_MODEL
EPS = 1e-6
DT_MIN = 0.001
DT_MAX = 0.1
A_INIT_MIN = 1.0
A_INIT_MAX = 16.0

kernel_name = "hybrid_stickbreaking_ssd_gated_block"


def rms_norm(x, w):
    xf = x.astype(jnp.float32)
    y = xf * lax.rsqrt(jnp.mean(xf * xf, axis=-1, keepdims=True) + EPS)
    return (y * w.astype(jnp.float32)).astype(x.dtype)


def stick_breaking_attention(q, k, v):
    s_len = q.shape[2]
    scale = q.shape[-1] ** -0.5
    outs = []
    for blk in range(s_len // Q_BLOCK):
        start = blk * Q_BLOCK
        end = start + Q_BLOCK
        qb = q[:, :, start:end]
        kb = k[:, :, :end]
        vb = v[:, :, :end]
        z = jnp.einsum('bhqd,bhkd->bhqk', qb, kb) * scale
        t_idx = start + jnp.arange(Q_BLOCK)[:, None]
        s_idx = jnp.arange(end)[None, :]
        mask = s_idx < t_idx
        log_beta = jax.nn.log_sigmoid(z)
        log_one_minus = jnp.where(mask, jax.nn.log_sigmoid(-z), 0.0)
        later = lax.cumsum(log_one_minus, axis=3, reverse=True) - log_one_minus
        a = jnp.where(mask, jnp.exp(log_beta + later), 0.0)
        outs.append(jnp.einsum('bhqk,bhkd->bhqd', a, vb))
    return jnp.concatenate(outs, axis=2)


def causal_depthwise_conv(x, w, b):
    c = x.shape[-1]
    y = lax.conv_general_dilated(
        x, w[:, None, :].astype(x.dtype), window_strides=(1,),
        padding=[(SSD_CONV - 1, 0)],
        dimension_numbers=('NWC', 'WIO', 'NWC'),
        feature_group_count=c)
    return y + b.astype(x.dtype)


def ssd_chunked(x, dt, a, bm, cm):
    b, s, g, hg, p = x.shape
    n = bm.shape[-1]
    nc = s // SSD_CHUNK
    x = x.reshape(b, nc, SSD_CHUNK, g, hg, p)
    dt = dt.reshape(b, nc, SSD_CHUNK, g, hg)
    bm = bm.reshape(b, nc, SSD_CHUNK, g, n)
    cm = cm.reshape(b, nc, SSD_CHUNK, g, n)
    a_cs = jnp.cumsum(dt * a, axis=2)
    xdt = x * dt[..., None]
    l_idx = jnp.arange(SSD_CHUNK)
    causal = (l_idx[:, None] >= l_idx[None, :])[:, :, None, None]
    seg = a_cs[:, :, :, None] - a_cs[:, :, None, :]
    decay = jnp.exp(jnp.where(causal, seg, -jnp.inf))
    cb = jnp.einsum('bclgn,bcsgn->bclsg', cm, bm)
    y_diag = jnp.einsum('bclsgh,bcsghp->bclghp', cb[..., None] * decay, xdt)
    decay_to_end = jnp.exp(a_cs[:, :, -1:] - a_cs)
    states = jnp.einsum('bclgn,bclghp->bcghpn', bm, xdt * decay_to_end[..., None])
    chunk_decay = jnp.exp(a_cs[:, :, -1])

    def step(h_prev, inp):
        st, dec = inp
        return h_prev * dec[..., None, None] + st, h_prev

    h0 = jnp.zeros((b, g, hg, p, n), jnp.float32)
    _, h_in = lax.scan(step, h0, (jnp.moveaxis(states, 1, 0), jnp.moveaxis(chunk_decay, 1, 0)))
    h_in = jnp.moveaxis(h_in, 0, 1)
    y_off = jnp.einsum('bclgn,bcghpn->bclghp', cm, h_in) * jnp.exp(a_cs)[..., None]
    return (y_diag + y_off).reshape(b, s, g, hg, p)


def hybrid_layer(x, norm_w, w_in, conv_w, conv_b, dt_bias, a_log, d_skip,
                 ssm_norm_w, w_attn_out, w_ssm_out, w_o):
    b, s, _ = x.shape
    f32 = jnp.float32
    h = rms_norm(x, norm_w)
    proj = jnp.einsum('bsd,de->bse', h, w_in)
    split_points = [int(v) for v in np.cumsum(PROJ_SPLITS)[:-1]]
    q, k, v, z_a, z_s, xbc, dt_raw, gate_raw = jnp.split(proj, split_points, axis=-1)

    def to_heads(t):
        return t.reshape(b, s, SB_HEADS, SB_HEAD_DIM).transpose(0, 2, 1, 3).astype(f32)

    o = stick_breaking_attention(to_heads(q), to_heads(k), to_heads(v))
    o = o.transpose(0, 2, 1, 3).reshape(b, s, SB_WIDTH)
    y_a = (o * jax.nn.silu(z_a.astype(f32))).astype(x.dtype)
    y_a = jnp.einsum('bse,ed->bsd', y_a, w_attn_out)

    xbc = jax.nn.silu(causal_depthwise_conv(xbc, conv_w, conv_b))
    xs, bm, cm = jnp.split(xbc, [SSD_WIDTH, SSD_WIDTH + SSD_BC_WIDTH], axis=-1)
    xs = xs.reshape(b, s, SSD_GROUPS, SSD_HEADS_PER_GROUP, SSD_HEAD_DIM).astype(f32)
    bm = bm.reshape(b, s, SSD_GROUPS, SSD_STATE).astype(f32)
    cm = cm.reshape(b, s, SSD_GROUPS, SSD_STATE).astype(f32)
    dt = jax.nn.softplus(dt_raw.astype(f32) + dt_bias.astype(f32))
    dt = dt.reshape(b, s, SSD_GROUPS, SSD_HEADS_PER_GROUP)
    a = -jnp.exp(a_log.astype(f32)).reshape(SSD_GROUPS, SSD_HEADS_PER_GROUP)
    y = ssd_chunked(xs, dt, a, bm, cm)
    y = y + xs * d_skip.astype(f32).reshape(SSD_GROUPS, SSD_HEADS_PER_GROUP)[..., None]
    y = y.reshape(b, s, SSD_WIDTH) * jax.nn.silu(z_s.astype(f32))
    yg = y.reshape(b, s, SSD_GROUPS, SSD_WIDTH // SSD_GROUPS)
    yg = yg * lax.rsqrt(jnp.mean(yg * yg, axis=-1, keepdims=True) + EPS)
    y = yg.reshape(b, s, SSD_WIDTH) * ssm_norm_w.astype(f32)
    y_s = jnp.einsum('bse,ed->bsd', y.astype(x.dtype), w_ssm_out)

    g_a, g_s = jnp.split(jax.nn.sigmoid(gate_raw.astype(f32)), N_BRANCHES, axis=-1)
    merged = (g_a * y_a.astype(f32) + g_s * y_s.astype(f32)).astype(x.dtype)
    return x + jnp.einsum('bsd,de->bse', merged, w_o)


def setup_inputs(seed: int = 0) -> dict:
    key = jax.random.key(seed)
    ks = jax.random.split(key, 14)
    f32 = jnp.float32
    x = jax.random.normal(ks[0], (BATCH, SEQ, D_MODEL), f32)
    norm_w = 1.0 + 0.02 * jax.random.normal(ks[1], (DEPTH, D_MODEL), f32)
    w_in = jax.random.normal(ks[2], (DEPTH, D_MODEL, D_PROJ), f32) * D_MODEL ** -0.5
    conv_w = jax.random.normal(ks[3], (DEPTH, SSD_CONV, SSD_CONV_DIM), f32) * SSD_CONV ** -0.5
    conv_b = 0.02 * jax.random.normal(ks[4], (DEPTH, SSD_CONV_DIM), f32)
    u = jax.random.uniform(ks[5], (DEPTH, SSD_HEADS), f32)
    dt0 = jnp.exp(u * (math.log(DT_MAX) - math.log(DT_MIN)) + math.log(DT_MIN))
    dt_bias = dt0 + jnp.log(-jnp.expm1(-dt0))
    a_log = jnp.log(jax.random.uniform(ks[6], (DEPTH, SSD_HEADS), f32, A_INIT_MIN, A_INIT_MAX))
    d_skip = 1.0 + 0.02 * jax.random.normal(ks[7], (DEPTH, SSD_HEADS), f32)
    ssm_norm_w = 1.0 + 0.02 * jax.random.normal(ks[8], (DEPTH, SSD_WIDTH), f32)
    w_attn_out = jax.random.normal(ks[9], (DEPTH, SB_WIDTH, D_MODEL), f32) * SB_WIDTH ** -0.5
    w_ssm_out = jax.random.normal(ks[10], (DEPTH, SSD_WIDTH, D_MODEL), f32) * SSD_WIDTH ** -0.5
    w_o = jax.random.normal(ks[11], (DEPTH, D_MODEL, D_MODEL), f32) * D_MODEL ** -0.5
    final_norm_w = 1.0 + 0.02 * jax.random.normal(ks[12], (D_MODEL,), f32)
    return {"x": x, "norm_w": norm_w, "w_in": w_in, "conv_w": conv_w, "conv_b": conv_b,
            "dt_bias": dt_bias, "a_log": a_log, "d_skip": d_skip, "ssm_norm_w": ssm_norm_w,
            "w_attn_out": w_attn_out, "w_ssm_out": w_ssm_out, "w_o": w_o,
            "final_norm_w": final_norm_w}


def reference(x, norm_w, w_in, conv_w, conv_b, dt_bias, a_log, d_skip, ssm_norm_w,
              w_attn_out, w_ssm_out, w_o, final_norm_w):
    h = x
    for layer in range(DEPTH):
        h = hybrid_layer(h, norm_w[layer], w_in[layer], conv_w[layer], conv_b[layer],
                         dt_bias[layer], a_log[layer], d_skip[layer], ssm_norm_w[layer],
                         w_attn_out[layer], w_ssm_out[layer], w_o[layer])
    return rms_norm(h, final_norm_w)
```

```python
import functools
import math

import jax
import jax.numpy as jnp
from jax import lax
from jax.experimental import pallas as pl
from jax.experimental.pallas import tpu as pltpu

F32 = jnp.float32
BF16 = jnp.bfloat16

LANES = 128
D_MODEL = 1024
SB_HEADS = 16
SB_HEAD_DIM = 64
SB_WIDTH = SB_HEADS * SB_HEAD_DIM
SSD_WIDTH = 2 * D_MODEL
SSD_HEAD_DIM = 64
SSD_HEADS = SSD_WIDTH // SSD_HEAD_DIM
SSD_GROUPS = 4
SSD_GROUP_WIDTH = SSD_WIDTH // SSD_GROUPS
SSD_STATE = 128
SSD_CONV = 4
SSD_CHUNK = 128
SSD_BC_WIDTH = SSD_GROUPS * SSD_STATE
SSD_CONV_DIM = SSD_WIDTH + 2 * SSD_BC_WIDTH
EPS = 1e-6

Q_OFF = 0
K_OFF = Q_OFF + SB_WIDTH
V_OFF = K_OFF + SB_WIDTH
ZA_OFF = V_OFF + SB_WIDTH
ZS_OFF = ZA_OFF + SB_WIDTH
XBC_OFF = ZS_OFF + SSD_WIDTH
GA_OFF = XBC_OFF + SSD_CONV_DIM
GS_OFF = GA_OFF + D_MODEL
N_MAIN = GS_OFF + D_MODEL
DT_SRC_OFF = XBC_OFF + SSD_CONV_DIM
GATE_SRC_OFF = DT_SRC_OFF + SSD_HEADS

VMEM_LIMIT = 48 * 1024 * 1024

IN_TM = 1024
IN_TN = 1024
ATTN_BLK = 256
OUT_TM = 512


def _dot(a, b):
    return jnp.dot(a, b, preferred_element_type=F32)


def _dot_nt(a, b):
    return lax.dot_general(a, b, (((1,), (1,)), ((), ())), preferred_element_type=F32)


def _split2(v):
    hi = v.astype(BF16)
    lo = (v - hi.astype(F32)).astype(BF16)
    return hi, lo


def _split3(v):
    hi = v.astype(BF16)
    r = v - hi.astype(F32)
    mid = r.astype(BF16)
    lo = (r - mid.astype(F32)).astype(BF16)
    return hi, mid, lo


def _silu(v):
    return v * jax.nn.sigmoid(v)


def _inproj_kernel(x_ref, nw_ref, w_ref, wdt_ref, o_ref, dt_ref, h_ref):
    @pl.when(pl.program_id(1) == 0)
    def _():
        x = x_ref[...]
        ms = jnp.mean(x * x, axis=-1, keepdims=True)
        h = ((x * lax.rsqrt(ms + EPS)) * nw_ref[...]).astype(BF16)
        h_ref[...] = h
        dt_ref[...] = _dot(h, wdt_ref[...])

    o_ref[...] = _dot(h_ref[...], w_ref[...]).astype(o_ref.dtype)


def _in_proj(x2, norm_w, w_main, w_dt):
    t = x2.shape[0]
    tm = min(IN_TM, t)
    return pl.pallas_call(
        _inproj_kernel,
        grid=(t // tm, N_MAIN // IN_TN),
        in_specs=[
            pl.BlockSpec((tm, D_MODEL), lambda i, j: (i, 0)),
            pl.BlockSpec((1, D_MODEL), lambda i, j: (0, 0)),
            pl.BlockSpec((D_MODEL, IN_TN), lambda i, j: (0, j)),
            pl.BlockSpec((D_MODEL, LANES), lambda i, j: (0, 0)),
        ],
        out_specs=[
            pl.BlockSpec((tm, IN_TN), lambda i, j: (i, j)),
            pl.BlockSpec((tm, LANES), lambda i, j: (i, 0)),
        ],
        out_shape=[
            jax.ShapeDtypeStruct((t, N_MAIN), BF16),
            jax.ShapeDtypeStruct((t, LANES), F32),
        ],
        scratch_shapes=[pltpu.VMEM((tm, D_MODEL), BF16)],
        compiler_params=pltpu.CompilerParams(
            dimension_semantics=("parallel", "arbitrary"), vmem_limit_bytes=VMEM_LIMIT),
        name="in_proj",
    )(x2, norm_w, w_main, w_dt)


def _attn_kernel(q_ref, k_ref, v_ref, za_ref, o_ref, acc_ref, c_ref, *, seq, blk):
    nblk = seq // blk
    reps = blk // LANES
    lane = lax.broadcasted_iota(jnp.int32, (1, LANES), 1)
    first = lane < SB_HEAD_DIM
    row = lax.broadcasted_iota(jnp.int32, (blk, blk), 0)
    col = lax.broadcasted_iota(jnp.int32, (blk, blk), 1)
    strict = col < row
    tri = jnp.where(row > col, 1.0, 0.0).astype(BF16)
    ones = jnp.ones((blk, LANES), BF16)

    def split_heads(m):
        zero = jnp.zeros_like(m)
        return jnp.where(first, m, zero), jnp.where(first, zero, m)

    def q_body(qi, carry):
        qoff = pl.multiple_of(qi * blk, blk)
        qh = split_heads(q_ref[pl.ds(qoff, blk), :])
        acc_ref[...] = jnp.zeros_like(acc_ref)
        c_ref[...] = jnp.zeros_like(c_ref)

        def block(koff, diag):
            k = k_ref[pl.ds(koff, blk), :]
            vh = split_heads(v_ref[pl.ds(koff, blk), :])
            contrib = None
            for hh in range(2):
                z = _dot_nt(qh[hh], k)
                sp = jnp.maximum(z, 0.0) + jnp.log2(1.0 + jnp.exp2(-jnp.abs(z)))
                log_beta = z - sp
                spm = jnp.where(strict, sp, 0.0) if diag else sp
                hi, lo = _split2(spm)
                within = _dot(hi, tri) + _dot(lo, tri)
                total = _dot(hi, ones) + _dot(lo, ones)
                c = c_ref[hh]
                c_full = c if reps == 1 else jnp.concatenate([c] * reps, axis=1)
                p = jnp.exp2(log_beta - (within + c_full))
                if diag:
                    p = jnp.where(strict, p, 0.0)
                c_ref[hh] = c + total
                pv = _dot(p.astype(BF16), vh[hh])
                contrib = pv if contrib is None else contrib + pv
            acc_ref[...] += contrib

        block(qoff, True)

        def k_body(i, carry2):
            block(pl.multiple_of((qi - 1 - i) * blk, blk), False)
            return carry2

        lax.fori_loop(0, qi, k_body, 0)
        za = za_ref[pl.ds(qoff, blk), :].astype(F32)
        o_ref[pl.ds(qoff, blk), :] = (acc_ref[...] * _silu(za)).astype(o_ref.dtype)
        return carry

    lax.fori_loop(0, nblk, q_body, 0)


def _sb_attention(proj, batch, seq):
    blk = min(ATTN_BLK, seq)
    pairs = SB_WIDTH // LANES

    def col_spec(off):
        return pl.BlockSpec((seq, LANES), lambda b, p: (b, off // LANES + p))

    return pl.pallas_call(
        functools.partial(_attn_kernel, seq=seq, blk=blk),
        grid=(batch, pairs),
        in_specs=[col_spec(Q_OFF), col_spec(K_OFF), col_spec(V_OFF), col_spec(ZA_OFF)],
        out_specs=pl.BlockSpec((seq, LANES), lambda b, p: (b, p)),
        out_shape=jax.ShapeDtypeStruct((batch * seq, SB_WIDTH), BF16),
        scratch_shapes=[pltpu.VMEM((blk, LANES), F32), pltpu.VMEM((2, blk, LANES), F32)],
        compiler_params=pltpu.CompilerParams(
            dimension_semantics=("parallel", "parallel"), vmem_limit_bytes=VMEM_LIMIT),
        name="sb_attn",
    )(proj, proj, proj, proj)


def _ssd_kernel(xbc_ref, zs_ref, dtr_ref, cw_ref, cb_ref, dtb_ref, alog_ref, dsk_ref, nw_ref,
                exp_ref, o_ref, xpad_ref, h_ref, y_ref):
    L = SSD_CHUNK
    pad = 8

    @pl.when(pl.program_id(1) == 0)
    def _():
        xpad_ref[0:pad, :] = jnp.zeros((pad, SSD_CONV_DIM), F32)
        h_ref[...] = jnp.zeros_like(h_ref)

    x = xbc_ref[...].astype(F32)
    xpad_ref[pad:pad + L, :] = x
    cw = cw_ref[...]
    acc = cb_ref[...] + cw[SSD_CONV - 1:SSD_CONV, :] * x
    for k in range(SSD_CONV - 1):
        shift = SSD_CONV - 1 - k
        acc = acc + cw[k:k + 1, :] * xpad_ref[pad - shift:pad - shift + L, :]
    xpad_ref[0:pad, :] = x[L - pad:L, :]
    u = _silu(acc)
    xs = u[:, :SSD_WIDTH]

    dtin = dtr_ref[...] + dtb_ref[...]
    dt = jnp.maximum(dtin, 0.0) + jnp.log(1.0 + jnp.exp(-jnp.abs(dtin)))
    d_a = dt * (-jnp.exp(alog_ref[...]))
    row = lax.broadcasted_iota(jnp.int32, (L, L), 0)
    col = lax.broadcasted_iota(jnp.int32, (L, L), 1)
    causal = row >= col
    ltri = jnp.where(causal, 1.0, 0.0).astype(BF16)
    utri = jnp.where(row <= col, 1.0, 0.0).astype(BF16)
    a_cs = sum(_dot(ltri, p) for p in _split3(d_a))
    a_cs_t = sum(_dot(p, utri) for p in _split3(d_a.T))

    expand = exp_ref[...]

    def per_lane(v):
        return sum(_dot(p, expand) for p in _split3(v))

    dt_full = per_lane(dt)
    ea_full = per_lane(jnp.exp(a_cs))
    dte_full = per_lane(jnp.exp(a_cs[L - 1:L, :] - a_cs))
    xdt = xs * dt_full
    xdt_b = xdt.astype(BF16)
    xdte_b = (xdt * dte_full).astype(BF16)

    lane = lax.broadcasted_iota(jnp.int32, (1, LANES), 1)
    first = lane < SSD_HEAD_DIM
    heads_per_group = SSD_HEADS // SSD_GROUPS
    for g in range(SSD_GROUPS):
        gs = slice(g * SSD_GROUP_WIDTH, (g + 1) * SSD_GROUP_WIDTH)
        b_off = SSD_WIDTH + g * SSD_STATE
        c_off = SSD_WIDTH + SSD_BC_WIDTH + g * SSD_STATE
        bg = u[:, b_off:b_off + SSD_STATE]
        cg_b = u[:, c_off:c_off + SSD_STATE].astype(BF16)
        cb = _dot_nt(cg_b, bg.astype(BF16))
        states = _dot(bg.T.astype(BF16), xdte_b[:, gs])
        h_in = h_ref[g]
        y_off = _dot(cg_b, h_in.astype(BF16)) * ea_full[:, gs]
        h_ref[g] = h_in * ea_full[L - 1:L, gs] + states
        for j in range(heads_per_group // 2):
            pc = g * SSD_GROUP_WIDTH + j * LANES
            xpair = xdt_b[:, pc:pc + LANES]
            zero = jnp.zeros_like(xpair)
            xh = (jnp.where(first, xpair, zero), jnp.where(first, zero, xpair))
            ypair = None
            for hh in range(2):
                h = g * heads_per_group + 2 * j + hh
                seg = a_cs[:, h:h + 1] - a_cs_t[h:h + 1, :]
                m = jnp.where(causal, cb * jnp.exp(seg), 0.0).astype(BF16)
                yh = _dot(m, xh[hh])
                ypair = yh if ypair is None else ypair + yh
            y_ref[:, pc:pc + LANES] = ypair
        y = y_ref[:, gs] + y_off + xs[:, gs] * dsk_ref[:, gs]
        y = y * _silu(zs_ref[:, gs].astype(F32))
        ms = jnp.mean(y * y, axis=-1, keepdims=True)
        o_ref[:, gs] = (y * lax.rsqrt(ms + EPS) * nw_ref[:, gs]).astype(o_ref.dtype)


def _ssd(proj, dt_raw, conv_w, conv_b, dt_bias, a_log, d_skip, ssm_norm_w, batch, seq):
    L = SSD_CHUNK
    nc = seq // L
    expand = (jnp.arange(SSD_WIDTH)[None, :] // SSD_HEAD_DIM == jnp.arange(LANES)[:, None]).astype(BF16)

    def pad_heads(v):
        return jnp.pad(v.astype(F32), (0, LANES - SSD_HEADS)).reshape(1, LANES)

    def const(shape):
        return pl.BlockSpec(shape, lambda b, c: (0, 0))

    return pl.pallas_call(
        _ssd_kernel,
        grid=(batch, nc),
        in_specs=[
            pl.BlockSpec((L, SSD_CONV_DIM), lambda b, c: (b * nc + c, XBC_OFF // SSD_CONV_DIM)),
            pl.BlockSpec((L, SSD_WIDTH), lambda b, c: (b * nc + c, ZS_OFF // SSD_WIDTH)),
            pl.BlockSpec((L, LANES), lambda b, c: (b * nc + c, 0)),
            const((SSD_CONV, SSD_CONV_DIM)),
            const((1, SSD_CONV_DIM)),
            const((1, LANES)),
            const((1, LANES)),
            const((1, SSD_WIDTH)),
            const((1, SSD_WIDTH)),
            const((LANES, SSD_WIDTH)),
        ],
        out_specs=pl.BlockSpec((L, SSD_WIDTH), lambda b, c: (b * nc + c, 0)),
        out_shape=jax.ShapeDtypeStruct((batch * seq, SSD_WIDTH), BF16),
        scratch_shapes=[
            pltpu.VMEM((8 + L, SSD_CONV_DIM), F32),
            pltpu.VMEM((SSD_GROUPS, SSD_STATE, SSD_GROUP_WIDTH), F32),
            pltpu.VMEM((L, SSD_WIDTH), F32),
        ],
        compiler_params=pltpu.CompilerParams(
            dimension_semantics=("parallel", "arbitrary"), vmem_limit_bytes=VMEM_LIMIT),
        name="ssd",
    )(proj, proj, dt_raw, conv_w.astype(F32), conv_b.astype(F32).reshape(1, -1), pad_heads(dt_bias),
      pad_heads(a_log), jnp.repeat(d_skip.astype(F32), SSD_HEAD_DIM).reshape(1, -1),
      ssm_norm_w.astype(F32).reshape(1, -1), expand)


def _outproj_kernel(ya_ref, yn_ref, ga_ref, gs_ref, x_ref, wa_ref, ws_ref, wo_ref, fw_ref, o_ref,
                    *, final_norm):
    y_a = _dot(ya_ref[...], wa_ref[...])
    y_s = _dot(yn_ref[...], ws_ref[...])
    merged = (jax.nn.sigmoid(ga_ref[...].astype(F32)) * y_a
              + jax.nn.sigmoid(gs_ref[...].astype(F32)) * y_s)
    out = x_ref[...] + _dot(merged.astype(BF16), wo_ref[...])
    if final_norm:
        ms = jnp.mean(out * out, axis=-1, keepdims=True)
        out = (out * lax.rsqrt(ms + EPS)) * fw_ref[...]
    o_ref[...] = out


def _out_proj(ya, yn, proj, x2, w_attn_out, w_ssm_out, w_o, final_w, final_norm):
    t = x2.shape[0]
    tm = min(OUT_TM, t)

    def const(shape):
        return pl.BlockSpec(shape, lambda i: (0, 0))

    return pl.pallas_call(
        functools.partial(_outproj_kernel, final_norm=final_norm),
        grid=(t // tm,),
        in_specs=[
            pl.BlockSpec((tm, SB_WIDTH), lambda i: (i, 0)),
            pl.BlockSpec((tm, SSD_WIDTH), lambda i: (i, 0)),
            pl.BlockSpec((tm, D_MODEL), lambda i: (i, GA_OFF // D_MODEL)),
            pl.BlockSpec((tm, D_MODEL), lambda i: (i, GS_OFF // D_MODEL)),
            pl.BlockSpec((tm, D_MODEL), lambda i: (i, 0)),
            const((SB_WIDTH, D_MODEL)),
            const((SSD_WIDTH, D_MODEL)),
            const((D_MODEL, D_MODEL)),
            const((1, D_MODEL)),
        ],
        out_specs=pl.BlockSpec((tm, D_MODEL), lambda i: (i, 0)),
        out_shape=jax.ShapeDtypeStruct((t, D_MODEL), F32),
        compiler_params=pltpu.CompilerParams(
            dimension_semantics=("parallel",), vmem_limit_bytes=VMEM_LIMIT),
        name="out_proj",
    )(ya, yn, proj, proj, x2, w_attn_out.astype(BF16), w_ssm_out.astype(BF16), w_o.astype(BF16),
      final_w.astype(F32).reshape(1, -1))


def _hybrid_layer(x2, batch, seq, norm_w, w_in, conv_w, conv_b, dt_bias, a_log, d_skip, ssm_norm_w,
                  w_attn_out, w_ssm_out, w_o, final_w, final_norm):
    q_scale = SB_HEAD_DIM ** -0.5 * math.log2(math.e)
    w_main = jnp.concatenate(
        [w_in[:, :K_OFF] * q_scale, w_in[:, K_OFF:DT_SRC_OFF], w_in[:, GATE_SRC_OFF:]], axis=1).astype(BF16)
    w_dt = jnp.pad(w_in[:, DT_SRC_OFF:GATE_SRC_OFF], ((0, 0), (0, LANES - SSD_HEADS))).astype(BF16)
    proj, dt_raw = _in_proj(x2, norm_w.astype(F32).reshape(1, -1), w_main, w_dt)
    ya = _sb_attention(proj, batch, seq)
    yn = _ssd(proj, dt_raw, conv_w, conv_b, dt_bias, a_log, d_skip, ssm_norm_w, batch, seq)
    return _out_proj(ya, yn, proj, x2, w_attn_out, w_ssm_out, w_o, final_w, final_norm)


def kernel(x, norm_w, w_in, conv_w, conv_b, dt_bias, a_log, d_skip, ssm_norm_w, w_attn_out, w_ssm_out,
           w_o, final_norm_w):
    batch, seq, _ = x.shape
    depth = norm_w.shape[0]
    h = x.reshape(batch * seq, D_MODEL)
    for layer in range(depth):
        h = _hybrid_layer(h, batch, seq, norm_w[layer], w_in[layer], conv_w[layer], conv_b[layer],
                          dt_bias[layer], a_log[layer], d_skip[layer], ssm_norm_w[layer],
                          w_attn_out[layer], w_ssm_out[layer], w_o[layer], final_norm_w,
                          final_norm=(layer == depth - 1))
    return h.reshape(batch, seq, D_MODEL)
```

```python
import functools
import math

import jax
import jax.numpy as jnp
from jax import lax
from jax.experimental import pallas as pl
from jax.experimental.pallas import tpu as pltpu

F32 = jnp.float32
BF16 = jnp.bfloat16

LANES = 128
D_MODEL = 1024
SB_HEADS = 16
SB_HEAD_DIM = 64
SB_WIDTH = SB_HEADS * SB_HEAD_DIM
SSD_WIDTH = 2 * D_MODEL
SSD_HEAD_DIM = 64
SSD_HEADS = SSD_WIDTH // SSD_HEAD_DIM
SSD_GROUPS = 4
SSD_GROUP_WIDTH = SSD_WIDTH // SSD_GROUPS
SSD_STATE = 128
SSD_CONV = 4
SSD_CHUNK = 128
SSD_BC_WIDTH = SSD_GROUPS * SSD_STATE
SSD_CONV_DIM = SSD_WIDTH + 2 * SSD_BC_WIDTH
EPS = 1e-6

Q_OFF = 0
K_OFF = Q_OFF + SB_WIDTH
V_OFF = K_OFF + SB_WIDTH
ZA_OFF = V_OFF + SB_WIDTH
ZS_OFF = ZA_OFF + SB_WIDTH
XBC_OFF = ZS_OFF + SSD_WIDTH
GA_OFF = XBC_OFF + SSD_CONV_DIM
GS_OFF = GA_OFF + D_MODEL
N_MAIN = GS_OFF + D_MODEL
DT_SRC_OFF = XBC_OFF + SSD_CONV_DIM
GATE_SRC_OFF = DT_SRC_OFF + SSD_HEADS

VMEM_LIMIT = 48 * 1024 * 1024

IN_TM = 1024
IN_TN = 1024
ATTN_TQ = 128
ATTN_TK = 256
ATTN_PAIRS = 1
ATTN_SKIP_LOG2 = 256.0
OUT_TM = 512


def _dot(a, b):
    return jnp.dot(a, b, preferred_element_type=F32)


def _dot_nt(a, b):
    return lax.dot_general(a, b, (((1,), (1,)), ((), ())), preferred_element_type=F32)


def _split2(v):
    hi = v.astype(BF16)
    lo = (v - hi.astype(F32)).astype(BF16)
    return hi, lo


def _split3(v):
    hi = v.astype(BF16)
    r = v - hi.astype(F32)
    mid = r.astype(BF16)
    lo = (r - mid.astype(F32)).astype(BF16)
    return hi, mid, lo


def _silu(v):
    return v * jax.nn.sigmoid(v)


def _inproj_kernel(x_ref, nw_ref, w_ref, wdt_ref, o_ref, dt_ref, h_ref):
    @pl.when(pl.program_id(1) == 0)
    def _():
        x = x_ref[...]
        ms = jnp.mean(x * x, axis=-1, keepdims=True)
        h = ((x * lax.rsqrt(ms + EPS)) * nw_ref[...]).astype(BF16)
        h_ref[...] = h
        dt_ref[...] = _dot(h, wdt_ref[...])

    o_ref[...] = _dot(h_ref[...], w_ref[...]).astype(o_ref.dtype)


def _in_proj(x2, norm_w, w_main, w_dt):
    t = x2.shape[0]
    tm = min(IN_TM, t)
    return pl.pallas_call(
        _inproj_kernel,
        grid=(t // tm, N_MAIN // IN_TN),
        in_specs=[
            pl.BlockSpec((tm, D_MODEL), lambda i, j: (i, 0)),
            pl.BlockSpec((1, D_MODEL), lambda i, j: (0, 0)),
            pl.BlockSpec((D_MODEL, IN_TN), lambda i, j: (0, j)),
            pl.BlockSpec((D_MODEL, LANES), lambda i, j: (0, 0)),
        ],
        out_specs=[
            pl.BlockSpec((tm, IN_TN), lambda i, j: (i, j)),
            pl.BlockSpec((tm, LANES), lambda i, j: (i, 0)),
        ],
        out_shape=[
            jax.ShapeDtypeStruct((t, N_MAIN), BF16),
            jax.ShapeDtypeStruct((t, LANES), F32),
        ],
        scratch_shapes=[pltpu.VMEM((tm, D_MODEL), BF16)],
        compiler_params=pltpu.CompilerParams(
            dimension_semantics=("parallel", "arbitrary"), vmem_limit_bytes=VMEM_LIMIT),
        name="in_proj",
    )(x2, norm_w, w_main, w_dt)


def _attn_kernel(q_ref, k_ref, v_ref, za_ref, o_ref, qs_ref, kp_ref, vs_ref, acc_ref, c_ref,
                 *, seq, pairs):
    tq, tk, pad = ATTN_TQ, ATTN_TK, ATTN_TQ
    width = pairs * LANES
    lane = lax.broadcasted_iota(jnp.int32, (1, width), 1)
    first = (lane & SB_HEAD_DIM) == 0

    def split_heads(dst_ref, m, base):
        zero = jnp.zeros_like(m)
        dst_ref[0, base:base + seq, :] = jnp.where(first, m, zero)
        dst_ref[1, base:base + seq, :] = jnp.where(first, zero, m)

    split_heads(qs_ref, q_ref[...], 0)
    kp_ref[0:pad, :] = jnp.zeros((pad, width), BF16)
    kp_ref[pad:pad + seq, :] = k_ref[...]
    vs_ref[:, 0:pad, :] = jnp.zeros((2, pad, width), BF16)
    split_heads(vs_ref, v_ref[...], pad)

    row = lax.broadcasted_iota(jnp.int32, (tq, tk), 0)
    col = lax.broadcasted_iota(jnp.int32, (tq, tk), 1)
    before = col < row + (tk - tq)
    trow = lax.broadcasted_iota(jnp.int32, (tk, tk), 0)
    tcol = lax.broadcasted_iota(jnp.int32, (tk, tk), 1)
    tri = jnp.where(trow > tcol, 1.0, 0.0).astype(BF16)

    def window(qoff, koff, hp, hh, c_prev):
        cols = slice(hp * LANES, (hp + 1) * LANES)
        q = qs_ref[hh, pl.ds(qoff, tq), cols]
        k = kp_ref[pl.ds(koff, tk), cols]
        v = vs_ref[hh, pl.ds(koff, tk), cols]
        z = _dot_nt(q, k)
        sp = jnp.maximum(z, 0.0) + jnp.log2(1.0 + jnp.exp2(-jnp.abs(z)))
        log_beta = z - sp
        if c_prev is None:
            sp = jnp.where(before, sp, 0.0)
        hi, lo = _split2(sp)
        later = _dot(hi, tri) + _dot(lo, tri)
        total = jnp.broadcast_to(jnp.sum(sp, axis=-1, keepdims=True), (tq, LANES))
        if c_prev is None:
            p = jnp.where(before, jnp.exp2(log_beta - later), 0.0)
            c_new = total
        else:
            p = jnp.exp2(log_beta - (later + jnp.concatenate([c_prev] * (tk // LANES), axis=1)))
            c_new = c_prev + total
        return _dot(p.astype(BF16), v), c_new

    def step(jj, w, is_first):
        cmin = None
        for qb in range(2):
            qoff = pl.multiple_of((2 * jj + qb) * tq, tq)
            koff = pl.multiple_of((jj - w) * tk + qb * tq, tq)
            for hp in range(pairs):
                contrib = None
                for hh in range(2):
                    pv, c_new = window(qoff, koff, hp, hh, None if is_first else c_ref[qb, hp, hh])
                    c_ref[qb, hp, hh] = c_new
                    contrib = pv if contrib is None else contrib + pv
                    cmin = c_new if cmin is None else jnp.minimum(cmin, c_new)
                if is_first:
                    acc_ref[qb, hp] = contrib
                else:
                    acc_ref[qb, hp] += contrib
        return jnp.min(cmin)

    def q_body(jj, carry):
        cmin0 = step(jj, 0, True)

        def cond(state):
            w, cmin = state
            return jnp.logical_and(w <= jj, cmin < ATTN_SKIP_LOG2)

        def body(state):
            w, _ = state
            return w + 1, step(jj, w, False)

        lax.while_loop(cond, body, (jnp.int32(1), cmin0))
        for qb in range(2):
            qoff = pl.multiple_of((2 * jj + qb) * tq, tq)
            for hp in range(pairs):
                cols = slice(hp * LANES, (hp + 1) * LANES)
                za = za_ref[pl.ds(qoff, tq), cols].astype(F32)
                o_ref[pl.ds(qoff, tq), cols] = (acc_ref[qb, hp] * _silu(za)).astype(o_ref.dtype)
        return carry

    lax.fori_loop(0, seq // (2 * tq), q_body, 0)


def _sb_attention(proj, batch, seq):
    pairs = ATTN_PAIRS
    width = pairs * LANES
    assert seq % ATTN_TK == 0 and ATTN_TK == 2 * ATTN_TQ

    def col_spec(off):
        return pl.BlockSpec((seq, width), lambda b, p: (b, off // width + p))

    return pl.pallas_call(
        functools.partial(_attn_kernel, seq=seq, pairs=pairs),
        grid=(batch, SB_WIDTH // width),
        in_specs=[col_spec(Q_OFF), col_spec(K_OFF), col_spec(V_OFF), col_spec(ZA_OFF)],
        out_specs=pl.BlockSpec((seq, width), lambda b, p: (b, p)),
        out_shape=jax.ShapeDtypeStruct((batch * seq, SB_WIDTH), BF16),
        scratch_shapes=[
            pltpu.VMEM((2, seq, width), BF16),
            pltpu.VMEM((ATTN_TQ + seq, width), BF16),
            pltpu.VMEM((2, ATTN_TQ + seq, width), BF16),
            pltpu.VMEM((2, pairs, ATTN_TQ, LANES), F32),
            pltpu.VMEM((2, pairs, 2, ATTN_TQ, LANES), F32),
        ],
        compiler_params=pltpu.CompilerParams(
            dimension_semantics=("parallel", "parallel"), vmem_limit_bytes=VMEM_LIMIT),
        name="sb_attn",
    )(proj, proj, proj, proj)


def _ssd_kernel(xbc_ref, zs_ref, dtr_ref, cw_ref, cb_ref, dtb_ref, alog_ref, dsk_ref, nw_ref,
                exp_ref, o_ref, xpad_ref, h_ref, y_ref):
    L = SSD_CHUNK
    pad = 8

    @pl.when(pl.program_id(1) == 0)
    def _():
        xpad_ref[0:pad, :] = jnp.zeros((pad, SSD_CONV_DIM), F32)
        h_ref[...] = jnp.zeros_like(h_ref)

    x = xbc_ref[...].astype(F32)
    xpad_ref[pad:pad + L, :] = x
    cw = cw_ref[...]
    acc = cb_ref[...] + cw[SSD_CONV - 1:SSD_CONV, :] * x
    for k in range(SSD_CONV - 1):
        shift = SSD_CONV - 1 - k
        acc = acc + cw[k:k + 1, :] * xpad_ref[pad - shift:pad - shift + L, :]
    xpad_ref[0:pad, :] = x[L - pad:L, :]
    u = _silu(acc)
    xs = u[:, :SSD_WIDTH]

    dtin = dtr_ref[...] + dtb_ref[...]
    dt = jnp.maximum(dtin, 0.0) + jnp.log(1.0 + jnp.exp(-jnp.abs(dtin)))
    d_a = dt * (-jnp.exp(alog_ref[...]))
    row = lax.broadcasted_iota(jnp.int32, (L, L), 0)
    col = lax.broadcasted_iota(jnp.int32, (L, L), 1)
    causal = row >= col
    ltri = jnp.where(causal, 1.0, 0.0).astype(BF16)
    utri = jnp.where(row <= col, 1.0, 0.0).astype(BF16)
    a_cs = sum(_dot(ltri, p) for p in _split3(d_a))
    a_cs_t = sum(_dot(p, utri) for p in _split3(d_a.T))

    expand = exp_ref[...]

    def per_lane(v):
        return sum(_dot(p, expand) for p in _split3(v))

    dt_full = per_lane(dt)
    ea_full = per_lane(jnp.exp(a_cs))
    dte_full = per_lane(jnp.exp(a_cs[L - 1:L, :] - a_cs))
    xdt = xs * dt_full
    xdt_b = xdt.astype(BF16)
    xdte_b = (xdt * dte_full).astype(BF16)

    lane = lax.broadcasted_iota(jnp.int32, (1, LANES), 1)
    first = lane < SSD_HEAD_DIM
    heads_per_group = SSD_HEADS // SSD_GROUPS
    for g in range(SSD_GROUPS):
        gs = slice(g * SSD_GROUP_WIDTH, (g + 1) * SSD_GROUP_WIDTH)
        b_off = SSD_WIDTH + g * SSD_STATE
        c_off = SSD_WIDTH + SSD_BC_WIDTH + g * SSD_STATE
        bg = u[:, b_off:b_off + SSD_STATE]
        cg_b = u[:, c_off:c_off + SSD_STATE].astype(BF16)
        cb = _dot_nt(cg_b, bg.astype(BF16))
        states = _dot(bg.T.astype(BF16), xdte_b[:, gs])
        h_in = h_ref[g]
        y_off = _dot(cg_b, h_in.astype(BF16)) * ea_full[:, gs]
        h_ref[g] = h_in * ea_full[L - 1:L, gs] + states
        for j in range(heads_per_group // 2):
            pc = g * SSD_GROUP_WIDTH + j * LANES
            xpair = xdt_b[:, pc:pc + LANES]
            zero = jnp.zeros_like(xpair)
            xh = (jnp.where(first, xpair, zero), jnp.where(first, zero, xpair))
            ypair = None
            for hh in range(2):
                h = g * heads_per_group + 2 * j + hh
                seg = a_cs[:, h:h + 1] - a_cs_t[h:h + 1, :]
                m = jnp.where(causal, cb * jnp.exp(seg), 0.0).astype(BF16)
                yh = _dot(m, xh[hh])
                ypair = yh if ypair is None else ypair + yh
            y_ref[:, pc:pc + LANES] = ypair
        y = y_ref[:, gs] + y_off + xs[:, gs] * dsk_ref[:, gs]
        y = y * _silu(zs_ref[:, gs].astype(F32))
        ms = jnp.mean(y * y, axis=-1, keepdims=True)
        o_ref[:, gs] = (y * lax.rsqrt(ms + EPS) * nw_ref[:, gs]).astype(o_ref.dtype)


def _ssd(proj, dt_raw, conv_w, conv_b, dt_bias, a_log, d_skip, ssm_norm_w, batch, seq):
    L = SSD_CHUNK
    nc = seq // L
    expand = (jnp.arange(SSD_WIDTH)[None, :] // SSD_HEAD_DIM == jnp.arange(LANES)[:, None]).astype(BF16)

    def pad_heads(v):
        return jnp.pad(v.astype(F32), (0, LANES - SSD_HEADS)).reshape(1, LANES)

    def const(shape):
        return pl.BlockSpec(shape, lambda b, c: (0, 0))

    return pl.pallas_call(
        _ssd_kernel,
        grid=(batch, nc),
        in_specs=[
            pl.BlockSpec((L, SSD_CONV_DIM), lambda b, c: (b * nc + c, XBC_OFF // SSD_CONV_DIM)),
            pl.BlockSpec((L, SSD_WIDTH), lambda b, c: (b * nc + c, ZS_OFF // SSD_WIDTH)),
            pl.BlockSpec((L, LANES), lambda b, c: (b * nc + c, 0)),
            const((SSD_CONV, SSD_CONV_DIM)),
            const((1, SSD_CONV_DIM)),
            const((1, LANES)),
            const((1, LANES)),
            const((1, SSD_WIDTH)),
            const((1, SSD_WIDTH)),
            const((LANES, SSD_WIDTH)),
        ],
        out_specs=pl.BlockSpec((L, SSD_WIDTH), lambda b, c: (b * nc + c, 0)),
        out_shape=jax.ShapeDtypeStruct((batch * seq, SSD_WIDTH), BF16),
        scratch_shapes=[
            pltpu.VMEM((8 + L, SSD_CONV_DIM), F32),
            pltpu.VMEM((SSD_GROUPS, SSD_STATE, SSD_GROUP_WIDTH), F32),
            pltpu.VMEM((L, SSD_WIDTH), F32),
        ],
        compiler_params=pltpu.CompilerParams(
            dimension_semantics=("parallel", "arbitrary"), vmem_limit_bytes=VMEM_LIMIT),
        name="ssd",
    )(proj, proj, dt_raw, conv_w.astype(F32), conv_b.astype(F32).reshape(1, -1), pad_heads(dt_bias),
      pad_heads(a_log), jnp.repeat(d_skip.astype(F32), SSD_HEAD_DIM).reshape(1, -1),
      ssm_norm_w.astype(F32).reshape(1, -1), expand)


def _outproj_kernel(ya_ref, yn_ref, ga_ref, gs_ref, x_ref, wa_ref, ws_ref, wo_ref, fw_ref, o_ref,
                    *, final_norm):
    y_a = _dot(ya_ref[...], wa_ref[...])
    y_s = _dot(yn_ref[...], ws_ref[...])
    merged = (jax.nn.sigmoid(ga_ref[...].astype(F32)) * y_a
              + jax.nn.sigmoid(gs_ref[...].astype(F32)) * y_s)
    out = x_ref[...] + _dot(merged.astype(BF16), wo_ref[...])
    if final_norm:
        ms = jnp.mean(out * out, axis=-1, keepdims=True)
        out = (out * lax.rsqrt(ms + EPS)) * fw_ref[...]
    o_ref[...] = out


def _out_proj(ya, yn, proj, x2, w_attn_out, w_ssm_out, w_o, final_w, final_norm):
    t = x2.shape[0]
    tm = min(OUT_TM, t)

    def const(shape):
        return pl.BlockSpec(shape, lambda i: (0, 0))

    return pl.pallas_call(
        functools.partial(_outproj_kernel, final_norm=final_norm),
        grid=(t // tm,),
        in_specs=[
            pl.BlockSpec((tm, SB_WIDTH), lambda i: (i, 0)),
            pl.BlockSpec((tm, SSD_WIDTH), lambda i: (i, 0)),
            pl.BlockSpec((tm, D_MODEL), lambda i: (i, GA_OFF // D_MODEL)),
            pl.BlockSpec((tm, D_MODEL), lambda i: (i, GS_OFF // D_MODEL)),
            pl.BlockSpec((tm, D_MODEL), lambda i: (i, 0)),
            const((SB_WIDTH, D_MODEL)),
            const((SSD_WIDTH, D_MODEL)),
            const((D_MODEL, D_MODEL)),
            const((1, D_MODEL)),
        ],
        out_specs=pl.BlockSpec((tm, D_MODEL), lambda i: (i, 0)),
        out_shape=jax.ShapeDtypeStruct((t, D_MODEL), F32),
        compiler_params=pltpu.CompilerParams(
            dimension_semantics=("parallel",), vmem_limit_bytes=VMEM_LIMIT),
        name="out_proj",
    )(ya, yn, proj, proj, x2, w_attn_out.astype(BF16), w_ssm_out.astype(BF16), w_o.astype(BF16),
      final_w.astype(F32).reshape(1, -1))


def _hybrid_layer(x2, batch, seq, norm_w, w_in, conv_w, conv_b, dt_bias, a_log, d_skip, ssm_norm_w,
                  w_attn_out, w_ssm_out, w_o, final_w, final_norm):
    q_scale = SB_HEAD_DIM ** -0.5 * math.log2(math.e)
    w_main = jnp.concatenate(
        [w_in[:, :K_OFF] * q_scale, w_in[:, K_OFF:DT_SRC_OFF], w_in[:, GATE_SRC_OFF:]], axis=1).astype(BF16)
    w_dt = jnp.pad(w_in[:, DT_SRC_OFF:GATE_SRC_OFF], ((0, 0), (0, LANES - SSD_HEADS))).astype(BF16)
    proj, dt_raw = _in_proj(x2, norm_w.astype(F32).reshape(1, -1), w_main, w_dt)
    ya = _sb_attention(proj, batch, seq)
    yn = _ssd(proj, dt_raw, conv_w, conv_b, dt_bias, a_log, d_skip, ssm_norm_w, batch, seq)
    return _out_proj(ya, yn, proj, x2, w_attn_out, w_ssm_out, w_o, final_w, final_norm)


def kernel(x, norm_w, w_in, conv_w, conv_b, dt_bias, a_log, d_skip, ssm_norm_w, w_attn_out, w_ssm_out,
           w_o, final_norm_w):
    batch, seq, _ = x.shape
    depth = norm_w.shape[0]
    h = x.reshape(batch * seq, D_MODEL)
    for layer in range(depth):
        h = _hybrid_layer(h, batch, seq, norm_w[layer], w_in[layer], conv_w[layer], conv_b[layer],
                          dt_bias[layer], a_log[layer], d_skip[layer], ssm_norm_w[layer],
                          w_attn_out[layer], w_ssm_out[layer], w_o[layer], final_norm_w,
                          final_norm=(layer == depth - 1))
    return h.reshape(batch, seq, D_MODEL)
```

```python
import functools
import math

import jax
import jax.numpy as jnp
from jax import lax
from jax.experimental import pallas as pl
from jax.experimental.pallas import tpu as pltpu

F32 = jnp.float32
BF16 = jnp.bfloat16

LANES = 128
D_MODEL = 1024
SB_HEADS = 16
SB_HEAD_DIM = 64
SB_WIDTH = SB_HEADS * SB_HEAD_DIM
SSD_WIDTH = 2 * D_MODEL
SSD_HEAD_DIM = 64
SSD_HEADS = SSD_WIDTH // SSD_HEAD_DIM
SSD_GROUPS = 4
SSD_GROUP_WIDTH = SSD_WIDTH // SSD_GROUPS
SSD_STATE = 128
SSD_CONV = 4
SSD_CHUNK = 128
SSD_CONV_STRIP = 256
SSD_TAIL = 16
SSD_BC_WIDTH = SSD_GROUPS * SSD_STATE
SSD_CONV_DIM = SSD_WIDTH + 2 * SSD_BC_WIDTH
EPS = 1e-6

Q_OFF = 0
K_OFF = Q_OFF + SB_WIDTH
V_OFF = K_OFF + SB_WIDTH
ZA_OFF = V_OFF + SB_WIDTH
ZS_OFF = ZA_OFF + SB_WIDTH
XBC_OFF = ZS_OFF + SSD_WIDTH
GA_OFF = XBC_OFF + SSD_CONV_DIM
GS_OFF = GA_OFF + D_MODEL
N_MAIN = GS_OFF + D_MODEL
DT_SRC_OFF = XBC_OFF + SSD_CONV_DIM
GATE_SRC_OFF = DT_SRC_OFF + SSD_HEADS

VMEM_LIMIT = 48 * 1024 * 1024

IN_TM = 2048
IN_TN = 1024
ATTN_TQ = 128
ATTN_TK = 256
ATTN_PAIRS = 2
ATTN_SKIP_LOG2 = 256.0
ATTN_MASK_BIAS = -1e30
OUT_TM = 512


def _dot(a, b):
    return jnp.dot(a, b, preferred_element_type=F32)


def _dot_nt(a, b):
    return lax.dot_general(a, b, (((1,), (1,)), ((), ())), preferred_element_type=F32)


def _split2(v):
    hi = v.astype(BF16)
    lo = (v - hi.astype(F32)).astype(BF16)
    return hi, lo


def _silu(v):
    return v * jax.nn.sigmoid(v)


def _inproj_kernel(x_ref, nw_ref, w_ref, wdt_ref, o_ref, dt_ref, h_ref):
    @pl.when(pl.program_id(1) == 0)
    def _():
        x = x_ref[...]
        ms = jnp.mean(x * x, axis=-1, keepdims=True)
        h = ((x * lax.rsqrt(ms + EPS)) * nw_ref[...]).astype(BF16)
        h_ref[...] = h
        dt_ref[...] = _dot(h, wdt_ref[...])

    o_ref[...] = _dot(h_ref[...], w_ref[...]).astype(o_ref.dtype)


def _in_proj(x2, norm_w, w_main, w_dt):
    t = x2.shape[0]
    tm = min(IN_TM, t)
    return pl.pallas_call(
        _inproj_kernel,
        grid=(t // tm, N_MAIN // IN_TN),
        in_specs=[
            pl.BlockSpec((tm, D_MODEL), lambda i, j: (i, 0)),
            pl.BlockSpec((1, D_MODEL), lambda i, j: (0, 0)),
            pl.BlockSpec((D_MODEL, IN_TN), lambda i, j: (0, j)),
            pl.BlockSpec((D_MODEL, LANES), lambda i, j: (0, 0)),
        ],
        out_specs=[
            pl.BlockSpec((tm, IN_TN), lambda i, j: (i, j)),
            pl.BlockSpec((tm, LANES), lambda i, j: (i, 0)),
        ],
        out_shape=[
            jax.ShapeDtypeStruct((t, N_MAIN), BF16),
            jax.ShapeDtypeStruct((t, LANES), F32),
        ],
        scratch_shapes=[pltpu.VMEM((tm, D_MODEL), BF16)],
        compiler_params=pltpu.CompilerParams(
            dimension_semantics=("parallel", "arbitrary"), vmem_limit_bytes=VMEM_LIMIT),
        name="in_proj",
    )(x2, norm_w, w_main, w_dt)


def _attn_kernel(q_ref, k_ref, v_ref, za_ref, o_ref, qs_ref, kp_ref, vs_ref, bias_ref, sp0_ref, sp1_ref,
                 arg0_ref, arg1_ref, acc_ref, c_ref, *, seq, pairs):
    tq, tk, pad = ATTN_TQ, ATTN_TK, ATTN_TQ
    width = pairs * LANES
    njj = seq // (2 * tq)
    sp_refs = (sp0_ref, sp1_ref)
    arg_refs = (arg0_ref, arg1_ref)
    sign = jnp.uint32(0x80000000)
    lane = lax.broadcasted_iota(jnp.int32, (1, width), 1)
    first = (lane & SB_HEAD_DIM) == 0

    def split_heads(dst_ref, m, base):
        zero = jnp.zeros_like(m)
        dst_ref[0, base:base + seq, :] = jnp.where(first, m, zero)
        dst_ref[1, base:base + seq, :] = jnp.where(first, zero, m)

    split_heads(qs_ref, q_ref[...], 0)
    kp_ref[0:pad, :] = jnp.zeros((pad, width), BF16)
    kp_ref[pad:pad + seq, :] = k_ref[...]
    vs_ref[:, 0:pad, :] = jnp.zeros((2, pad, width), BF16)
    split_heads(vs_ref, v_ref[...], pad)
    acc_ref[...] = jnp.zeros_like(acc_ref)
    c_ref[...] = jnp.zeros_like(c_ref)

    row = lax.broadcasted_iota(jnp.int32, (tq, tk), 0)
    col = lax.broadcasted_iota(jnp.int32, (tq, tk), 1)
    bias_ref[0] = jnp.where(col < row + (tk - tq), 0.0, ATTN_MASK_BIAS)
    bias_ref[1] = jnp.zeros((tq, tk), F32)
    trow = lax.broadcasted_iota(jnp.int32, (tk, tk), 0)
    tcol = lax.broadcasted_iota(jnp.int32, (tk, tk), 1)
    tri = jnp.where(trow > tcol, 1.0, 0.0).astype(BF16)

    def offsets(jj, w, qb):
        qoff = pl.multiple_of((2 * jj + qb) * tq, tq)
        koff = pl.multiple_of((jj - w) * tk + qb * tq, tq)
        return qoff, koff

    groups = [(qb, hp) for qb in range(2) for hp in range(pairs)]

    def cols_of(hp):
        return slice(hp * LANES, (hp + 1) * LANES)

    def scores(jj, w, qb, hp):
        qoff, koff = offsets(jj, w, qb)
        k = kp_ref[pl.ds(koff, tk), cols_of(hp)]
        bias = bias_ref[jnp.minimum(w, 1)]
        return [_dot_nt(qs_ref[hh, pl.ds(qoff, tq), cols_of(hp)], k) + bias for hh in range(2)]

    def carried(w, slot, qb, hp, zs):
        keep = (w > 0).astype(F32)
        cmin = None
        for hh in range(2):
            z = zs[hh]
            neg_abs = pltpu.bitcast(pltpu.bitcast(z, jnp.uint32) | sign, F32)
            sp = jnp.maximum(z, 0.0) + jnp.log2(1.0 + jnp.exp2(neg_abs))
            sp_refs[slot][qb, hp, hh] = sp.astype(BF16)
            c_in = c_ref[qb, hp, hh] * keep
            arg_refs[slot][qb, hp, hh] = (z - sp) - jnp.concatenate([c_in] * (tk // LANES), axis=1)
            total = jnp.broadcast_to(jnp.sum(sp, axis=-1, keepdims=True), (tq, LANES))
            c_new = c_in + total
            c_ref[qb, hp, hh] = c_new
            cmin = c_new if cmin is None else jnp.minimum(cmin, c_new)
        return cmin

    def laters(slot, qb, hp):
        return [_dot(sp_refs[slot][qb, hp, hh], tri) for hh in range(2)]

    def weighted(jj, w, slot, qb, hp, lat):
        _, koff = offsets(jj, w, qb)
        contrib = None
        for hh in range(2):
            p = jnp.exp2(arg_refs[slot][qb, hp, hh] - lat[hh])
            pv = _dot(p.astype(BF16), vs_ref[hh, pl.ds(koff, tk), cols_of(hp)])
            contrib = pv if contrib is None else contrib + pv
        return contrib

    def stage_a_all(jj, w, slot):
        cmin = None
        for qb, hp in groups:
            c = carried(w, slot, qb, hp, scores(jj, w, qb, hp))
            cmin = c if cmin is None else jnp.minimum(cmin, c)
        return jnp.min(cmin)

    def finalize(jj):
        for qb in range(2):
            qoff, _ = offsets(jj, 0, qb)
            for hp in range(pairs):
                cols = slice(hp * LANES, (hp + 1) * LANES)
                za = za_ref[pl.ds(qoff, tq), cols].astype(F32)
                o_ref[pl.ds(qoff, tq), cols] = (acc_ref[qb, hp] * _silu(za)).astype(o_ref.dtype)

    def next_item(jj, w, cmin):
        more = jnp.logical_and(w < jj, cmin < ATTN_SKIP_LOG2)
        return jnp.where(more, jj, jj + 1), jnp.where(more, w + 1, 0)

    def tick(cj, cw, nj, nw, slot):
        aj = jnp.minimum(nj, njj - 1)
        keep = (cw > 0).astype(F32)
        pending = (laters(slot, *groups[0]), scores(aj, nw, *groups[0]))
        cmin = None
        for g, (qb, hp) in enumerate(groups):
            lat, zs = pending
            if g + 1 < len(groups):
                pending = (laters(slot, *groups[g + 1]), scores(aj, nw, *groups[g + 1]))
            contrib = weighted(cj, cw, slot, qb, hp, lat)
            c = carried(nw, 1 - slot, qb, hp, zs)
            cmin = c if cmin is None else jnp.minimum(cmin, c)
            acc_ref[qb, hp] = acc_ref[qb, hp] * keep + contrib
        cmin = jnp.min(cmin)

        @pl.when(nj != cj)
        def _():
            finalize(cj)

        return next_item(nj, nw, cmin)

    zero = jnp.int32(0)
    nj0, nw0 = next_item(zero, zero, stage_a_all(zero, zero, 0))

    def cond(state):
        return state[0] < njj

    def body(state):
        cj, cw, nj, nw = state
        n2j, n2w = tick(cj, cw, nj, nw, 0)

        def second_tick():
            n3j, n3w = tick(nj, nw, n2j, n2w, 1)
            return n2j, n2w, n3j, n3w

        return lax.cond(nj < njj, second_tick, lambda: (nj, nw, n2j, n2w))

    lax.while_loop(cond, body, (zero, zero, nj0, nw0))


def _sb_attention(proj, batch, seq):
    pairs = ATTN_PAIRS
    width = pairs * LANES
    tq, tk = ATTN_TQ, ATTN_TK
    assert seq % tk == 0 and tk == 2 * tq

    def col_spec(off):
        return pl.BlockSpec((seq, width), lambda b, p: (b, off // width + p))

    return pl.pallas_call(
        functools.partial(_attn_kernel, seq=seq, pairs=pairs),
        grid=(batch, SB_WIDTH // width),
        in_specs=[col_spec(Q_OFF), col_spec(K_OFF), col_spec(V_OFF), col_spec(ZA_OFF)],
        out_specs=pl.BlockSpec((seq, width), lambda b, p: (b, p)),
        out_shape=jax.ShapeDtypeStruct((batch * seq, SB_WIDTH), BF16),
        scratch_shapes=[
            pltpu.VMEM((2, seq, width), BF16),
            pltpu.VMEM((tq + seq, width), BF16),
            pltpu.VMEM((2, tq + seq, width), BF16),
            pltpu.VMEM((2, tq, tk), F32),
            pltpu.VMEM((2, pairs, 2, tq, tk), BF16),
            pltpu.VMEM((2, pairs, 2, tq, tk), BF16),
            pltpu.VMEM((2, pairs, 2, tq, tk), F32),
            pltpu.VMEM((2, pairs, 2, tq, tk), F32),
            pltpu.VMEM((2, pairs, tq, LANES), F32),
            pltpu.VMEM((2, pairs, 2, tq, LANES), F32),
        ],
        compiler_params=pltpu.CompilerParams(
            dimension_semantics=("parallel", "parallel"), vmem_limit_bytes=VMEM_LIMIT),
        name="sb_attn",
    )(proj, proj, proj, proj)


def _ssd_kernel(xbc_ref, zs_ref, dtr_ref, cw_ref, cb_ref, dtb_ref, alog_ref, dsk_ref, nw_ref,
                exp_ref, o_ref, xtail_ref, u_ref, h_ref, y_ref):
    L = SSD_CHUNK
    tail = SSD_TAIL

    @pl.when(pl.program_id(1) == 0)
    def _():
        xtail_ref[...] = jnp.zeros_like(xtail_ref)
        h_ref[...] = jnp.zeros_like(h_ref)

    srow = lax.broadcasted_iota(jnp.int32, (L, tail + L), 0)
    scol = lax.broadcasted_iota(jnp.int32, (L, tail + L), 1)
    picks = [jnp.where(scol == srow + (tail - (SSD_CONV - 1 - k)), 1.0, 0.0).astype(BF16)
             for k in range(SSD_CONV - 1)]
    for c0 in range(0, SSD_CONV_DIM, SSD_CONV_STRIP):
        cs = slice(c0, c0 + SSD_CONV_STRIP)
        x_b = xbc_ref[:, cs]
        xcat = jnp.concatenate([xtail_ref[:, cs], x_b], axis=0)
        acc = cb_ref[:, cs] + cw_ref[SSD_CONV - 1:SSD_CONV, cs] * x_b.astype(F32)
        for k in range(SSD_CONV - 1):
            acc = acc + cw_ref[k:k + 1, cs] * _dot(picks[k], xcat)
        u_ref[:, cs] = _silu(acc)
    xtail_ref[...] = xbc_ref[L - tail:L, :]

    dtin = dtr_ref[...] + dtb_ref[...]
    dt = jnp.maximum(dtin, 0.0) + jnp.log(1.0 + jnp.exp(-jnp.abs(dtin)))
    d_a = dt * (-jnp.exp(alog_ref[...]))
    row = lax.broadcasted_iota(jnp.int32, (L, L), 0)
    col = lax.broadcasted_iota(jnp.int32, (L, L), 1)
    causal = row >= col
    ltri = jnp.where(causal, 1.0, 0.0).astype(BF16)
    utri = jnp.where(row <= col, 1.0, 0.0).astype(BF16)
    a_cs = _dot(jnp.concatenate([ltri, ltri], axis=1),
                jnp.concatenate(_split2(d_a), axis=0))
    a_cs_t = _dot(jnp.concatenate(_split2(d_a.T), axis=1),
                  jnp.concatenate([utri, utri], axis=0))

    def stacked(v):
        return jnp.concatenate(_split2(v), axis=1)

    dt_hl = stacked(dt)
    ea_hl = stacked(jnp.exp(a_cs))
    dte_hl = stacked(jnp.exp(a_cs[L - 1:L, :] - a_cs))

    lane = lax.broadcasted_iota(jnp.int32, (1, LANES), 1)
    first = lane < SSD_HEAD_DIM
    heads_per_group = SSD_HEADS // SSD_GROUPS
    for g in range(SSD_GROUPS):
        gs = slice(g * SSD_GROUP_WIDTH, (g + 1) * SSD_GROUP_WIDTH)
        b_off = SSD_WIDTH + g * SSD_STATE
        c_off = SSD_WIDTH + SSD_BC_WIDTH + g * SSD_STATE
        expand = exp_ref[:, gs]
        xs = u_ref[:, gs]
        xdt = xs * _dot(dt_hl, expand)
        xdt_b = xdt.astype(BF16)
        xdte_b = (xdt * _dot(dte_hl, expand)).astype(BF16)
        ea_full = _dot(ea_hl, expand)
        bg = u_ref[:, b_off:b_off + SSD_STATE]
        cg_b = u_ref[:, c_off:c_off + SSD_STATE].astype(BF16)
        cb = _dot_nt(cg_b, bg.astype(BF16))
        states = _dot(bg.T.astype(BF16), xdte_b)
        h_in = h_ref[g]
        y_off = _dot(cg_b, h_in.astype(BF16)) * ea_full
        h_ref[g] = h_in * ea_full[L - 1:L, :] + states
        for j in range(heads_per_group // 2):
            pc = g * SSD_GROUP_WIDTH + j * LANES
            xpair = xdt_b[:, j * LANES:(j + 1) * LANES]
            zero = jnp.zeros_like(xpair)
            xh = (jnp.where(first, xpair, zero), jnp.where(first, zero, xpair))
            ypair = None
            for hh in range(2):
                h = g * heads_per_group + 2 * j + hh
                seg = a_cs[:, h:h + 1] - a_cs_t[h:h + 1, :]
                m = jnp.where(causal, cb * jnp.exp(seg), 0.0).astype(BF16)
                yh = _dot(m, xh[hh])
                ypair = yh if ypair is None else ypair + yh
            y_ref[:, pc:pc + LANES] = ypair
        y = y_ref[:, gs] + y_off + xs * dsk_ref[:, gs]
        y = y * _silu(zs_ref[:, gs].astype(F32))
        ms = jnp.mean(y * y, axis=-1, keepdims=True)
        o_ref[:, gs] = (y * lax.rsqrt(ms + EPS) * nw_ref[:, gs]).astype(o_ref.dtype)


def _ssd(proj, dt_raw, conv_w, conv_b, dt_bias, a_log, d_skip, ssm_norm_w, batch, seq):
    L = SSD_CHUNK
    nc = seq // L
    expand = (jnp.arange(SSD_WIDTH)[None, :] // SSD_HEAD_DIM == jnp.arange(LANES)[:, None]).astype(BF16)
    expand = jnp.concatenate([expand, expand], axis=0)

    def pad_heads(v):
        return jnp.pad(v.astype(F32), (0, LANES - SSD_HEADS)).reshape(1, LANES)

    def const(shape):
        return pl.BlockSpec(shape, lambda b, c: (0, 0))

    return pl.pallas_call(
        _ssd_kernel,
        grid=(batch, nc),
        in_specs=[
            pl.BlockSpec((L, SSD_CONV_DIM), lambda b, c: (b * nc + c, XBC_OFF // SSD_CONV_DIM)),
            pl.BlockSpec((L, SSD_WIDTH), lambda b, c: (b * nc + c, ZS_OFF // SSD_WIDTH)),
            pl.BlockSpec((L, LANES), lambda b, c: (b * nc + c, 0)),
            const((SSD_CONV, SSD_CONV_DIM)),
            const((1, SSD_CONV_DIM)),
            const((1, LANES)),
            const((1, LANES)),
            const((1, SSD_WIDTH)),
            const((1, SSD_WIDTH)),
            const((2 * LANES, SSD_WIDTH)),
        ],
        out_specs=pl.BlockSpec((L, SSD_WIDTH), lambda b, c: (b * nc + c, 0)),
        out_shape=jax.ShapeDtypeStruct((batch * seq, SSD_WIDTH), BF16),
        scratch_shapes=[
            pltpu.VMEM((SSD_TAIL, SSD_CONV_DIM), BF16),
            pltpu.VMEM((L, SSD_CONV_DIM), F32),
            pltpu.VMEM((SSD_GROUPS, SSD_STATE, SSD_GROUP_WIDTH), F32),
            pltpu.VMEM((L, SSD_WIDTH), F32),
        ],
        compiler_params=pltpu.CompilerParams(
            dimension_semantics=("parallel", "arbitrary"), vmem_limit_bytes=VMEM_LIMIT),
        name="ssd",
    )(proj, proj, dt_raw, conv_w.astype(F32), conv_b.astype(F32).reshape(1, -1), pad_heads(dt_bias),
      pad_heads(a_log), jnp.repeat(d_skip.astype(F32), SSD_HEAD_DIM).reshape(1, -1),
      ssm_norm_w.astype(F32).reshape(1, -1), expand)


def _outproj_kernel(ya_ref, yn_ref, ga_ref, gs_ref, x_ref, wa_ref, ws_ref, wo_ref, fw_ref, o_ref,
                    *, final_norm):
    y_a = _dot(ya_ref[...], wa_ref[...])
    y_s = _dot(yn_ref[...], ws_ref[...])
    merged = (jax.nn.sigmoid(ga_ref[...].astype(F32)) * y_a
              + jax.nn.sigmoid(gs_ref[...].astype(F32)) * y_s)
    out = x_ref[...] + _dot(merged.astype(BF16), wo_ref[...])
    if final_norm:
        ms = jnp.mean(out * out, axis=-1, keepdims=True)
        out = (out * lax.rsqrt(ms + EPS)) * fw_ref[...]
    o_ref[...] = out


def _out_proj(ya, yn, proj, x2, w_attn_out, w_ssm_out, w_o, final_w, final_norm):
    t = x2.shape[0]
    tm = min(OUT_TM, t)

    def const(shape):
        return pl.BlockSpec(shape, lambda i: (0, 0))

    return pl.pallas_call(
        functools.partial(_outproj_kernel, final_norm=final_norm),
        grid=(t // tm,),
        in_specs=[
            pl.BlockSpec((tm, SB_WIDTH), lambda i: (i, 0)),
            pl.BlockSpec((tm, SSD_WIDTH), lambda i: (i, 0)),
            pl.BlockSpec((tm, D_MODEL), lambda i: (i, GA_OFF // D_MODEL)),
            pl.BlockSpec((tm, D_MODEL), lambda i: (i, GS_OFF // D_MODEL)),
            pl.BlockSpec((tm, D_MODEL), lambda i: (i, 0)),
            const((SB_WIDTH, D_MODEL)),
            const((SSD_WIDTH, D_MODEL)),
            const((D_MODEL, D_MODEL)),
            const((1, D_MODEL)),
        ],
        out_specs=pl.BlockSpec((tm, D_MODEL), lambda i: (i, 0)),
        out_shape=jax.ShapeDtypeStruct((t, D_MODEL), F32),
        compiler_params=pltpu.CompilerParams(
            dimension_semantics=("parallel",), vmem_limit_bytes=VMEM_LIMIT),
        name="out_proj",
    )(ya, yn, proj, proj, x2, w_attn_out.astype(BF16), w_ssm_out.astype(BF16), w_o.astype(BF16),
      final_w.astype(F32).reshape(1, -1))


def _hybrid_layer(x2, batch, seq, norm_w, w_in, conv_w, conv_b, dt_bias, a_log, d_skip, ssm_norm_w,
                  w_attn_out, w_ssm_out, w_o, final_w, final_norm):
    q_scale = SB_HEAD_DIM ** -0.5 * math.log2(math.e)
    w_main = jnp.concatenate(
        [w_in[:, :K_OFF] * q_scale, w_in[:, K_OFF:DT_SRC_OFF], w_in[:, GATE_SRC_OFF:]], axis=1).astype(BF16)
    w_dt = jnp.pad(w_in[:, DT_SRC_OFF:GATE_SRC_OFF], ((0, 0), (0, LANES - SSD_HEADS))).astype(BF16)
    proj, dt_raw = _in_proj(x2, norm_w.astype(F32).reshape(1, -1), w_main, w_dt)
    ya = _sb_attention(proj, batch, seq)
    yn = _ssd(proj, dt_raw, conv_w, conv_b, dt_bias, a_log, d_skip, ssm_norm_w, batch, seq)
    return _out_proj(ya, yn, proj, x2, w_attn_out, w_ssm_out, w_o, final_w, final_norm)


def kernel(x, norm_w, w_in, conv_w, conv_b, dt_bias, a_log, d_skip, ssm_norm_w, w_attn_out, w_ssm_out,
           w_o, final_norm_w):
    batch, seq, _ = x.shape
    depth = norm_w.shape[0]
    h = x.reshape(batch * seq, D_MODEL)
    for layer in range(depth):
        h = _hybrid_layer(h, batch, seq, norm_w[layer], w_in[layer], conv_w[layer], conv_b[layer],
                          dt_bias[layer], a_log[layer], d_skip[layer], ssm_norm_w[layer],
                          w_attn_out[layer], w_ssm_out[layer], w_o[layer], final_norm_w,
                          final_norm=(layer == depth - 1))
    return h.reshape(batch, seq, D_MODEL)
```

```python
import functools
import math

import jax
import jax.numpy as jnp
from jax import lax
from jax.experimental import pallas as pl
from jax.experimental.pallas import tpu as pltpu

F32 = jnp.float32
BF16 = jnp.bfloat16

LANES = 128
D_MODEL = 1024
SB_HEADS = 16
SB_HEAD_DIM = 64
SB_WIDTH = SB_HEADS * SB_HEAD_DIM
SSD_WIDTH = 2 * D_MODEL
SSD_HEAD_DIM = 64
SSD_HEADS = SSD_WIDTH // SSD_HEAD_DIM
SSD_GROUPS = 4
SSD_GROUP_WIDTH = SSD_WIDTH // SSD_GROUPS
SSD_STATE = 128
SSD_CONV = 4
SSD_CHUNK = 128
SSD_CONV_STRIP = 256
SSD_TAIL = 16
SSD_BC_WIDTH = SSD_GROUPS * SSD_STATE
SSD_CONV_DIM = SSD_WIDTH + 2 * SSD_BC_WIDTH
EPS = 1e-6

Q_OFF = 0
K_OFF = Q_OFF + SB_WIDTH
V_OFF = K_OFF + SB_WIDTH
ZA_OFF = V_OFF + SB_WIDTH
ZS_OFF = ZA_OFF + SB_WIDTH
XBC_OFF = ZS_OFF + SSD_WIDTH
GA_OFF = XBC_OFF + SSD_CONV_DIM
GS_OFF = GA_OFF + D_MODEL
N_MAIN = GS_OFF + D_MODEL
DT_SRC_OFF = XBC_OFF + SSD_CONV_DIM
GATE_SRC_OFF = DT_SRC_OFF + SSD_HEADS

VMEM_LIMIT = 48 * 1024 * 1024

IN_TM = 2048
IN_TN = 1024
ATTN_TQ = 64
ATTN_TK = 256
ATTN_PAIRS = 4
ATTN_SKIP_LOG2 = 160.0
ATTN_MASK_BIAS = -1e30
OUT_TM = 512


def _dot(a, b):
    return jnp.dot(a, b, preferred_element_type=F32)


def _dot_nt(a, b):
    return lax.dot_general(a, b, (((1,), (1,)), ((), ())), preferred_element_type=F32)


def _split2(v):
    hi = v.astype(BF16)
    lo = (v - hi.astype(F32)).astype(BF16)
    return hi, lo


def _silu(v):
    return v * jax.nn.sigmoid(v)


def _inproj_kernel(x_ref, nw_ref, w_ref, wdt_ref, o_ref, dt_ref, h_ref):
    @pl.when(pl.program_id(1) == 0)
    def _():
        x = x_ref[...]
        ms = jnp.mean(x * x, axis=-1, keepdims=True)
        h = ((x * lax.rsqrt(ms + EPS)) * nw_ref[...]).astype(BF16)
        h_ref[...] = h
        dt_ref[...] = _dot(h, wdt_ref[...])

    o_ref[...] = _dot(h_ref[...], w_ref[...]).astype(o_ref.dtype)


def _in_proj(x2, norm_w, w_main, w_dt):
    t = x2.shape[0]
    tm = min(IN_TM, t)
    return pl.pallas_call(
        _inproj_kernel,
        grid=(t // tm, N_MAIN // IN_TN),
        in_specs=[
            pl.BlockSpec((tm, D_MODEL), lambda i, j: (i, 0)),
            pl.BlockSpec((1, D_MODEL), lambda i, j: (0, 0)),
            pl.BlockSpec((D_MODEL, IN_TN), lambda i, j: (0, j)),
            pl.BlockSpec((D_MODEL, LANES), lambda i, j: (0, 0)),
        ],
        out_specs=[
            pl.BlockSpec((tm, IN_TN), lambda i, j: (i, j)),
            pl.BlockSpec((tm, LANES), lambda i, j: (i, 0)),
        ],
        out_shape=[
            jax.ShapeDtypeStruct((t, N_MAIN), BF16),
            jax.ShapeDtypeStruct((t, LANES), F32),
        ],
        scratch_shapes=[pltpu.VMEM((tm, D_MODEL), BF16)],
        compiler_params=pltpu.CompilerParams(
            dimension_semantics=("parallel", "arbitrary"), vmem_limit_bytes=VMEM_LIMIT),
        name="in_proj",
    )(x2, norm_w, w_main, w_dt)


def _attn_kernel(q_ref, k_ref, v_ref, za_ref, o_ref, kp_ref, vp_ref, bias_ref, sp0_ref, sp1_ref,
                 arg0_ref, arg1_ref, acc_ref, c_ref, *, seq, pairs):
    tq, tk = ATTN_TQ, ATTN_TK
    pad = tk - tq
    nqb = tk // tq
    width = pairs * LANES
    njj = seq // tk
    sp_refs = (sp0_ref, sp1_ref)
    arg_refs = (arg0_ref, arg1_ref)
    sign = jnp.uint32(0x80000000)
    lane = lax.broadcasted_iota(jnp.int32, (1, LANES), 1)
    first = lane < SB_HEAD_DIM

    kp_ref[0:pad, :] = jnp.zeros((pad, width), BF16)
    kp_ref[pad:pad + seq, :] = k_ref[...]
    vp_ref[0:pad, :] = jnp.zeros((pad, width), BF16)
    vp_ref[pad:pad + seq, :] = v_ref[...]
    acc_ref[...] = jnp.zeros_like(acc_ref)
    c_ref[...] = jnp.zeros_like(c_ref)

    row = lax.broadcasted_iota(jnp.int32, (tq, LANES), 0)
    col = lax.broadcasted_iota(jnp.int32, (tq, LANES), 1)
    bias_ref[0] = jnp.where(col < row + (LANES - tq), 0.0, ATTN_MASK_BIAS)
    bias_ref[1] = jnp.zeros((tq, LANES), F32)
    trow = lax.broadcasted_iota(jnp.int32, (tk, tk), 0)
    tcol = lax.broadcasted_iota(jnp.int32, (tk, tk), 1)
    tri = jnp.where(trow > tcol, 1.0, 0.0).astype(BF16)

    def offsets(jj, w, qb):
        qoff = pl.multiple_of((nqb * jj + qb) * tq, tq)
        koff = pl.multiple_of((jj - w) * tk + qb * tq, tq)
        return qoff, koff

    groups = [(qb, hp) for qb in range(nqb) for hp in range(pairs)]

    def cols_of(hp):
        return slice(hp * LANES, (hp + 1) * LANES)

    def scores(jj, w, qb, hp):
        qoff, koff = offsets(jj, w, qb)
        k = kp_ref[pl.ds(koff, tk), cols_of(hp)]
        bias = bias_ref[jnp.minimum(w, 1)]
        q = q_ref[pl.ds(qoff, tq), cols_of(hp)]
        zero = jnp.zeros_like(q)
        zs = [_dot_nt(qh, k) for qh in (jnp.where(first, q, zero), jnp.where(first, zero, q))]
        return [jnp.concatenate([z[:, :tk - LANES], z[:, tk - LANES:] + bias], axis=1) for z in zs]

    def carried(w, slot, qb, hp, zs):
        wsel = jnp.minimum(w, 1)
        cmin = None
        for hh in range(2):
            z = zs[hh]
            neg_abs = pltpu.bitcast(pltpu.bitcast(z, jnp.uint32) | sign, F32)
            sp = jnp.maximum(z, 0.0) + jnp.log2(1.0 + jnp.exp2(neg_abs))
            sp_refs[slot][qb, hp, hh] = sp.astype(BF16)
            c_in = c_ref[wsel, qb, hp, hh]
            arg_refs[slot][qb, hp, hh] = (z - sp) - jnp.concatenate([c_in] * (tk // LANES), axis=1)
            total = jnp.broadcast_to(jnp.sum(sp, axis=-1, keepdims=True), (tq, LANES))
            c_new = c_in + total
            c_ref[1, qb, hp, hh] = c_new
            cmin = c_new if cmin is None else jnp.minimum(cmin, c_new)
        return cmin

    def laters(slot, qb, hp):
        return [_dot(sp_refs[slot][qb, hp, hh], tri) for hh in range(2)]

    def weighted(jj, w, slot, qb, hp, lat):
        _, koff = offsets(jj, w, qb)
        v = vp_ref[pl.ds(koff, tk), cols_of(hp)]
        pv = [_dot(jnp.exp2(arg_refs[slot][qb, hp, hh] - lat[hh]).astype(BF16), v) for hh in range(2)]
        return jnp.where(first, pv[0], pv[1])

    def stage_a_all(jj, w, slot):
        cmin = None
        for qb, hp in groups:
            c = carried(w, slot, qb, hp, scores(jj, w, qb, hp))
            cmin = c if cmin is None else jnp.minimum(cmin, c)
        return jnp.min(cmin)

    def finalize(jj):
        for qb, hp in groups:
            qoff, _ = offsets(jj, 0, qb)
            za = za_ref[pl.ds(qoff, tq), cols_of(hp)].astype(F32)
            o_ref[pl.ds(qoff, tq), cols_of(hp)] = (acc_ref[1, qb, hp] * _silu(za)).astype(o_ref.dtype)

    def next_item(jj, w, cmin):
        more = jnp.logical_and(w < jj, cmin < ATTN_SKIP_LOG2)
        return jnp.where(more, jj, jj + 1), jnp.where(more, w + 1, 0)

    def tick(cj, cw, nj, nw, slot):
        aj = jnp.minimum(nj, njj - 1)
        csel = jnp.minimum(cw, 1)
        pending = (laters(slot, *groups[0]), scores(aj, nw, *groups[0]))
        cmin = None
        for g, (qb, hp) in enumerate(groups):
            lat, zs = pending
            if g + 1 < len(groups):
                pending = (laters(slot, *groups[g + 1]), scores(aj, nw, *groups[g + 1]))
            contrib = weighted(cj, cw, slot, qb, hp, lat)
            c = carried(nw, 1 - slot, qb, hp, zs)
            cmin = c if cmin is None else jnp.minimum(cmin, c)
            acc_ref[1, qb, hp] = acc_ref[csel, qb, hp] + contrib
        cmin = jnp.min(cmin)

        @pl.when(nj != cj)
        def _():
            finalize(cj)

        return next_item(nj, nw, cmin)

    zero = jnp.int32(0)
    nj0, nw0 = next_item(zero, zero, stage_a_all(zero, zero, 0))

    def cond(state):
        return state[0] < njj

    def body(state):
        cj, cw, nj, nw = state
        n2j, n2w = tick(cj, cw, nj, nw, 0)

        def second_tick():
            n3j, n3w = tick(nj, nw, n2j, n2w, 1)
            return n2j, n2w, n3j, n3w

        return lax.cond(nj < njj, second_tick, lambda: (nj, nw, n2j, n2w))

    lax.while_loop(cond, body, (zero, zero, nj0, nw0))


def _sb_attention(proj, batch, seq):
    pairs = ATTN_PAIRS
    width = pairs * LANES
    tq, tk = ATTN_TQ, ATTN_TK
    nqb = tk // tq
    assert seq % tk == 0 and tk % tq == 0 and tq <= LANES

    def col_spec(off):
        return pl.BlockSpec((seq, width), lambda b, p: (b, off // width + p))

    return pl.pallas_call(
        functools.partial(_attn_kernel, seq=seq, pairs=pairs),
        grid=(batch, SB_WIDTH // width),
        in_specs=[col_spec(Q_OFF), col_spec(K_OFF), col_spec(V_OFF), col_spec(ZA_OFF)],
        out_specs=pl.BlockSpec((seq, width), lambda b, p: (b, p)),
        out_shape=jax.ShapeDtypeStruct((batch * seq, SB_WIDTH), BF16),
        scratch_shapes=[
            pltpu.VMEM((tk - tq + seq, width), BF16),
            pltpu.VMEM((tk - tq + seq, width), BF16),
            pltpu.VMEM((2, tq, LANES), F32),
            pltpu.VMEM((nqb, pairs, 2, tq, tk), BF16),
            pltpu.VMEM((nqb, pairs, 2, tq, tk), BF16),
            pltpu.VMEM((nqb, pairs, 2, tq, tk), F32),
            pltpu.VMEM((nqb, pairs, 2, tq, tk), F32),
            pltpu.VMEM((2, nqb, pairs, tq, LANES), F32),
            pltpu.VMEM((2, nqb, pairs, 2, tq, LANES), F32),
        ],
        compiler_params=pltpu.CompilerParams(
            dimension_semantics=("parallel", "parallel"), vmem_limit_bytes=VMEM_LIMIT),
        name="sb_attn",
    )(proj, proj, proj, proj)


def _ssd_kernel(xbc_ref, zs_ref, dtr_ref, cw_ref, cb_ref, dtb_ref, alog_ref, dsk_ref, nw_ref,
                exp_ref, o_ref, xtail_ref, u_ref, h_ref, y_ref):
    L = SSD_CHUNK
    tail = SSD_TAIL

    @pl.when(pl.program_id(1) == 0)
    def _():
        xtail_ref[...] = jnp.zeros_like(xtail_ref)
        h_ref[...] = jnp.zeros_like(h_ref)

    srow = lax.broadcasted_iota(jnp.int32, (L, tail + L), 0)
    scol = lax.broadcasted_iota(jnp.int32, (L, tail + L), 1)
    picks = [jnp.where(scol == srow + (tail - (SSD_CONV - 1 - k)), 1.0, 0.0).astype(BF16)
             for k in range(SSD_CONV - 1)]
    for c0 in range(0, SSD_CONV_DIM, SSD_CONV_STRIP):
        cs = slice(c0, c0 + SSD_CONV_STRIP)
        x_b = xbc_ref[:, cs]
        xcat = jnp.concatenate([xtail_ref[:, cs], x_b], axis=0)
        acc = cb_ref[:, cs] + cw_ref[SSD_CONV - 1:SSD_CONV, cs] * x_b.astype(F32)
        for k in range(SSD_CONV - 1):
            acc = acc + cw_ref[k:k + 1, cs] * _dot(picks[k], xcat)
        u_ref[:, cs] = _silu(acc)
    xtail_ref[...] = xbc_ref[L - tail:L, :]

    dtin = dtr_ref[...] + dtb_ref[...]
    dt = jnp.maximum(dtin, 0.0) + jnp.log(1.0 + jnp.exp(-jnp.abs(dtin)))
    d_a = dt * (-jnp.exp(alog_ref[...]))
    row = lax.broadcasted_iota(jnp.int32, (L, L), 0)
    col = lax.broadcasted_iota(jnp.int32, (L, L), 1)
    causal = row >= col
    ltri = jnp.where(causal, 1.0, 0.0).astype(BF16)
    utri = jnp.where(row <= col, 1.0, 0.0).astype(BF16)
    a_cs = _dot(jnp.concatenate([ltri, ltri], axis=1),
                jnp.concatenate(_split2(d_a), axis=0))
    a_cs_t = _dot(jnp.concatenate(_split2(d_a.T), axis=1),
                  jnp.concatenate([utri, utri], axis=0))

    def stacked(v):
        return jnp.concatenate(_split2(v), axis=1)

    dt_hl = stacked(dt)
    ea_hl = stacked(jnp.exp(a_cs))
    dte_hl = stacked(jnp.exp(a_cs[L - 1:L, :] - a_cs))

    lane = lax.broadcasted_iota(jnp.int32, (1, LANES), 1)
    first = lane < SSD_HEAD_DIM
    heads_per_group = SSD_HEADS // SSD_GROUPS
    for g in range(SSD_GROUPS):
        gs = slice(g * SSD_GROUP_WIDTH, (g + 1) * SSD_GROUP_WIDTH)
        b_off = SSD_WIDTH + g * SSD_STATE
        c_off = SSD_WIDTH + SSD_BC_WIDTH + g * SSD_STATE
        expand = exp_ref[:, gs]
        xs = u_ref[:, gs]
        xdt = xs * _dot(dt_hl, expand)
        xdt_b = xdt.astype(BF16)
        xdte_b = (xdt * _dot(dte_hl, expand)).astype(BF16)
        ea_full = _dot(ea_hl, expand)
        bg = u_ref[:, b_off:b_off + SSD_STATE]
        cg_b = u_ref[:, c_off:c_off + SSD_STATE].astype(BF16)
        cb = _dot_nt(cg_b, bg.astype(BF16))
        states = _dot(bg.T.astype(BF16), xdte_b)
        h_in = h_ref[g]
        y_off = _dot(cg_b, h_in.astype(BF16)) * ea_full
        h_ref[g] = h_in * ea_full[L - 1:L, :] + states
        for j in range(heads_per_group // 2):
            pc = g * SSD_GROUP_WIDTH + j * LANES
            xpair = xdt_b[:, j * LANES:(j + 1) * LANES]
            zero = jnp.zeros_like(xpair)
            xh = (jnp.where(first, xpair, zero), jnp.where(first, zero, xpair))
            ypair = None
            for hh in range(2):
                h = g * heads_per_group + 2 * j + hh
                seg = a_cs[:, h:h + 1] - a_cs_t[h:h + 1, :]
                m = jnp.where(causal, cb * jnp.exp(seg), 0.0).astype(BF16)
                yh = _dot(m, xh[hh])
                ypair = yh if ypair is None else ypair + yh
            y_ref[:, pc:pc + LANES] = ypair
        y = y_ref[:, gs] + y_off + xs * dsk_ref[:, gs]
        y = y * _silu(zs_ref[:, gs].astype(F32))
        ms = jnp.mean(y * y, axis=-1, keepdims=True)
        o_ref[:, gs] = (y * lax.rsqrt(ms + EPS) * nw_ref[:, gs]).astype(o_ref.dtype)


def _ssd(proj, dt_raw, conv_w, conv_b, dt_bias, a_log, d_skip, ssm_norm_w, batch, seq):
    L = SSD_CHUNK
    nc = seq // L
    expand = (jnp.arange(SSD_WIDTH)[None, :] // SSD_HEAD_DIM == jnp.arange(LANES)[:, None]).astype(BF16)
    expand = jnp.concatenate([expand, expand], axis=0)

    def pad_heads(v):
        return jnp.pad(v.astype(F32), (0, LANES - SSD_HEADS)).reshape(1, LANES)

    def const(shape):
        return pl.BlockSpec(shape, lambda b, c: (0, 0))

    return pl.pallas_call(
        _ssd_kernel,
        grid=(batch, nc),
        in_specs=[
            pl.BlockSpec((L, SSD_CONV_DIM), lambda b, c: (b * nc + c, XBC_OFF // SSD_CONV_DIM)),
            pl.BlockSpec((L, SSD_WIDTH), lambda b, c: (b * nc + c, ZS_OFF // SSD_WIDTH)),
            pl.BlockSpec((L, LANES), lambda b, c: (b * nc + c, 0)),
            const((SSD_CONV, SSD_CONV_DIM)),
            const((1, SSD_CONV_DIM)),
            const((1, LANES)),
            const((1, LANES)),
            const((1, SSD_WIDTH)),
            const((1, SSD_WIDTH)),
            const((2 * LANES, SSD_WIDTH)),
        ],
        out_specs=pl.BlockSpec((L, SSD_WIDTH), lambda b, c: (b * nc + c, 0)),
        out_shape=jax.ShapeDtypeStruct((batch * seq, SSD_WIDTH), BF16),
        scratch_shapes=[
            pltpu.VMEM((SSD_TAIL, SSD_CONV_DIM), BF16),
            pltpu.VMEM((L, SSD_CONV_DIM), F32),
            pltpu.VMEM((SSD_GROUPS, SSD_STATE, SSD_GROUP_WIDTH), F32),
            pltpu.VMEM((L, SSD_WIDTH), F32),
        ],
        compiler_params=pltpu.CompilerParams(
            dimension_semantics=("parallel", "arbitrary"), vmem_limit_bytes=VMEM_LIMIT),
        name="ssd",
    )(proj, proj, dt_raw, conv_w.astype(F32), conv_b.astype(F32).reshape(1, -1), pad_heads(dt_bias),
      pad_heads(a_log), jnp.repeat(d_skip.astype(F32), SSD_HEAD_DIM).reshape(1, -1),
      ssm_norm_w.astype(F32).reshape(1, -1), expand)


def _outproj_kernel(ya_ref, yn_ref, ga_ref, gs_ref, x_ref, wa_ref, ws_ref, wo_ref, fw_ref, o_ref,
                    *, final_norm):
    y_a = _dot(ya_ref[...], wa_ref[...])
    y_s = _dot(yn_ref[...], ws_ref[...])
    merged = (jax.nn.sigmoid(ga_ref[...].astype(F32)) * y_a
              + jax.nn.sigmoid(gs_ref[...].astype(F32)) * y_s)
    out = x_ref[...] + _dot(merged.astype(BF16), wo_ref[...])
    if final_norm:
        ms = jnp.mean(out * out, axis=-1, keepdims=True)
        out = (out * lax.rsqrt(ms + EPS)) * fw_ref[...]
    o_ref[...] = out


def _out_proj(ya, yn, proj, x2, w_attn_out, w_ssm_out, w_o, final_w, final_norm):
    t = x2.shape[0]
    tm = min(OUT_TM, t)

    def const(shape):
        return pl.BlockSpec(shape, lambda i: (0, 0))

    return pl.pallas_call(
        functools.partial(_outproj_kernel, final_norm=final_norm),
        grid=(t // tm,),
        in_specs=[
            pl.BlockSpec((tm, SB_WIDTH), lambda i: (i, 0)),
            pl.BlockSpec((tm, SSD_WIDTH), lambda i: (i, 0)),
            pl.BlockSpec((tm, D_MODEL), lambda i: (i, GA_OFF // D_MODEL)),
            pl.BlockSpec((tm, D_MODEL), lambda i: (i, GS_OFF // D_MODEL)),
            pl.BlockSpec((tm, D_MODEL), lambda i: (i, 0)),
            const((SB_WIDTH, D_MODEL)),
            const((SSD_WIDTH, D_MODEL)),
            const((D_MODEL, D_MODEL)),
            const((1, D_MODEL)),
        ],
        out_specs=pl.BlockSpec((tm, D_MODEL), lambda i: (i, 0)),
        out_shape=jax.ShapeDtypeStruct((t, D_MODEL), F32),
        compiler_params=pltpu.CompilerParams(
            dimension_semantics=("parallel",), vmem_limit_bytes=VMEM_LIMIT),
        name="out_proj",
    )(ya, yn, proj, proj, x2, w_attn_out.astype(BF16), w_ssm_out.astype(BF16), w_o.astype(BF16),
      final_w.astype(F32).reshape(1, -1))


def _hybrid_layer(x2, batch, seq, norm_w, w_in, conv_w, conv_b, dt_bias, a_log, d_skip, ssm_norm_w,
                  w_attn_out, w_ssm_out, w_o, final_w, final_norm):
    q_scale = SB_HEAD_DIM ** -0.5 * math.log2(math.e)
    w_main = jnp.concatenate(
        [w_in[:, :K_OFF] * q_scale, w_in[:, K_OFF:DT_SRC_OFF], w_in[:, GATE_SRC_OFF:]], axis=1).astype(BF16)
    w_dt = jnp.pad(w_in[:, DT_SRC_OFF:GATE_SRC_OFF], ((0, 0), (0, LANES - SSD_HEADS))).astype(BF16)
    proj, dt_raw = _in_proj(x2, norm_w.astype(F32).reshape(1, -1), w_main, w_dt)
    ya = _sb_attention(proj, batch, seq)
    yn = _ssd(proj, dt_raw, conv_w, conv_b, dt_bias, a_log, d_skip, ssm_norm_w, batch, seq)
    return _out_proj(ya, yn, proj, x2, w_attn_out, w_ssm_out, w_o, final_w, final_norm)


def kernel(x, norm_w, w_in, conv_w, conv_b, dt_bias, a_log, d_skip, ssm_norm_w, w_attn_out, w_ssm_out,
           w_o, final_norm_w):
    batch, seq, _ = x.shape
    depth = norm_w.shape[0]
    h = x.reshape(batch * seq, D_MODEL)
    for layer in range(depth):
        h = _hybrid_layer(h, batch, seq, norm_w[layer], w_in[layer], conv_w[layer], conv_b[layer],
                          dt_bias[layer], a_log[layer], d_skip[layer], ssm_norm_w[layer],
                          w_attn_out[layer], w_ssm_out[layer], w_o[layer], final_norm_w,
                          final_norm=(layer == depth - 1))
    return h.reshape(batch, seq, D_MODEL)
```

```python
import functools
import math

import jax
import jax.numpy as jnp
from jax import lax
from jax.experimental import pallas as pl
from jax.experimental.pallas import tpu as pltpu

F32 = jnp.float32
BF16 = jnp.bfloat16

LANES = 128
D_MODEL = 1024
SB_HEADS = 16
SB_HEAD_DIM = 64
SB_WIDTH = SB_HEADS * SB_HEAD_DIM
SSD_WIDTH = 2 * D_MODEL
SSD_HEAD_DIM = 64
SSD_HEADS = SSD_WIDTH // SSD_HEAD_DIM
SSD_GROUPS = 4
SSD_GROUP_WIDTH = SSD_WIDTH // SSD_GROUPS
SSD_STATE = 128
SSD_CONV = 4
SSD_CHUNK = 128
SSD_STEP_CHUNKS = 2
SSD_CONV_STRIP = 256
SSD_TAIL = 16
SSD_BC_WIDTH = SSD_GROUPS * SSD_STATE
SSD_CONV_DIM = SSD_WIDTH + 2 * SSD_BC_WIDTH
EPS = 1e-6

Q_OFF = 0
K_OFF = Q_OFF + SB_WIDTH
V_OFF = K_OFF + SB_WIDTH
ZA_OFF = V_OFF + SB_WIDTH
ZS_OFF = ZA_OFF + SB_WIDTH
XBC_OFF = ZS_OFF + SSD_WIDTH
GA_OFF = XBC_OFF + SSD_CONV_DIM
GS_OFF = GA_OFF + D_MODEL
N_MAIN = GS_OFF + D_MODEL
DT_SRC_OFF = XBC_OFF + SSD_CONV_DIM
GATE_SRC_OFF = DT_SRC_OFF + SSD_HEADS

VMEM_LIMIT = 48 * 1024 * 1024

IN_TM = 2048
IN_TN = 1024
ATTN_TQ = 64
ATTN_TK = 256
ATTN_PAIRS = 4
ATTN_SKIP_LOG2 = 160.0
ATTN_MASK_BIAS = -1e30
OUT_TM = 512


def _dot(a, b):
    return jnp.dot(a, b, preferred_element_type=F32)


def _dot_nt(a, b):
    return lax.dot_general(a, b, (((1,), (1,)), ((), ())), preferred_element_type=F32)


def _split2(v):
    hi = v.astype(BF16)
    lo = (v - hi.astype(F32)).astype(BF16)
    return hi, lo


def _silu(v):
    return v * jax.nn.sigmoid(v)


def _inproj_kernel(x_ref, nw_ref, w_ref, wdt_ref, o_ref, dt_ref, h_ref):
    @pl.when(pl.program_id(1) == 0)
    def _():
        x = x_ref[...]
        ms = jnp.mean(x * x, axis=-1, keepdims=True)
        h = ((x * lax.rsqrt(ms + EPS)) * nw_ref[...]).astype(BF16)
        h_ref[...] = h
        dt_ref[...] = _dot_nt(h, wdt_ref[...])

    o_ref[...] = _dot_nt(h_ref[...], w_ref[...]).astype(o_ref.dtype)


def _in_proj(x2, norm_w, w_main_t, w_dt_t):
    t = x2.shape[0]
    tm = min(IN_TM, t)
    return pl.pallas_call(
        _inproj_kernel,
        grid=(t // tm, N_MAIN // IN_TN),
        in_specs=[
            pl.BlockSpec((tm, D_MODEL), lambda i, j: (i, 0)),
            pl.BlockSpec((1, D_MODEL), lambda i, j: (0, 0)),
            pl.BlockSpec((IN_TN, D_MODEL), lambda i, j: (j, 0)),
            pl.BlockSpec((LANES, D_MODEL), lambda i, j: (0, 0)),
        ],
        out_specs=[
            pl.BlockSpec((tm, IN_TN), lambda i, j: (i, j)),
            pl.BlockSpec((tm, LANES), lambda i, j: (i, 0)),
        ],
        out_shape=[
            jax.ShapeDtypeStruct((t, N_MAIN), BF16),
            jax.ShapeDtypeStruct((t, LANES), F32),
        ],
        scratch_shapes=[pltpu.VMEM((tm, D_MODEL), BF16)],
        compiler_params=pltpu.CompilerParams(
            dimension_semantics=("parallel", "arbitrary"), vmem_limit_bytes=VMEM_LIMIT),
        name="in_proj",
    )(x2, norm_w, w_main_t, w_dt_t)


def _attn_kernel(q_ref, k_ref, v_ref, za_ref, o_ref, kp_ref, vp_ref, bias_ref, sp0_ref, sp1_ref,
                 arg0_ref, arg1_ref, acc_ref, c_ref, *, seq, pairs):
    tq, tk = ATTN_TQ, ATTN_TK
    pad = tk - tq
    nqb = tk // tq
    width = pairs * LANES
    njj = seq // tk
    sp_refs = (sp0_ref, sp1_ref)
    arg_refs = (arg0_ref, arg1_ref)
    sign = jnp.uint32(0x80000000)
    lane = lax.broadcasted_iota(jnp.int32, (1, LANES), 1)
    first = lane < SB_HEAD_DIM

    kp_ref[0:pad, :] = jnp.zeros((pad, width), BF16)
    kp_ref[pad:pad + seq, :] = k_ref[...]
    vp_ref[0:pad, :] = jnp.zeros((pad, width), BF16)
    vp_ref[pad:pad + seq, :] = v_ref[...]

    @pl.when(jnp.logical_and(pl.program_id(0) == 0, pl.program_id(1) == 0))
    def _():
        acc_ref[0] = jnp.zeros(acc_ref.shape[1:], F32)
        c_ref[0] = jnp.zeros(c_ref.shape[1:], F32)


    row = lax.broadcasted_iota(jnp.int32, (tq, LANES), 0)
    col = lax.broadcasted_iota(jnp.int32, (tq, LANES), 1)
    bias_ref[0] = jnp.where(col < row + (LANES - tq), 0.0, ATTN_MASK_BIAS)
    bias_ref[1] = jnp.zeros((tq, LANES), F32)
    trow = lax.broadcasted_iota(jnp.int32, (tk, tk), 0)
    tcol = lax.broadcasted_iota(jnp.int32, (tk, tk), 1)
    tri = jnp.where(trow > tcol, 1.0, 0.0).astype(BF16)

    def offsets(jj, w, qb):
        qoff = pl.multiple_of((nqb * jj + qb) * tq, tq)
        koff = pl.multiple_of((jj - w) * tk + qb * tq, tq)
        return qoff, koff

    groups = [(qb, hp) for qb in range(nqb) for hp in range(pairs)]

    def cols_of(hp):
        return slice(hp * LANES, (hp + 1) * LANES)

    def scores(jj, w, qb, hp):
        qoff, koff = offsets(jj, w, qb)
        k = kp_ref[pl.ds(koff, tk), cols_of(hp)]
        bias = bias_ref[jnp.minimum(w, 1)]
        q = q_ref[pl.ds(qoff, tq), cols_of(hp)]
        zero = jnp.zeros_like(q)
        zs = [_dot_nt(qh, k) for qh in (jnp.where(first, q, zero), jnp.where(first, zero, q))]
        return [jnp.concatenate([z[:, :tk - LANES], z[:, tk - LANES:] + bias], axis=1) for z in zs]

    def carried(w, slot, qb, hp, zs):
        wsel = jnp.minimum(w, 1)
        cmin = None
        for hh in range(2):
            z = zs[hh]
            neg_abs = pltpu.bitcast(pltpu.bitcast(z, jnp.uint32) | sign, F32)
            sp = jnp.maximum(z, 0.0) + jnp.log2(1.0 + jnp.exp2(neg_abs))
            sp_refs[slot][qb, hp, hh] = sp.astype(BF16)
            c_in = c_ref[wsel, qb, hp, hh]
            arg_refs[slot][qb, hp, hh] = (z - sp) - jnp.concatenate([c_in] * (tk // LANES), axis=1)
            total = jnp.broadcast_to(jnp.sum(sp, axis=-1, keepdims=True), (tq, LANES))
            c_new = c_in + total
            c_ref[1, qb, hp, hh] = c_new
            cmin = c_new if cmin is None else jnp.minimum(cmin, c_new)
        return cmin

    def laters(slot, qb, hp):
        return [_dot(sp_refs[slot][qb, hp, hh], tri) for hh in range(2)]

    def weighted(jj, w, slot, qb, hp, lat):
        _, koff = offsets(jj, w, qb)
        v = vp_ref[pl.ds(koff, tk), cols_of(hp)]
        pv = [_dot(jnp.exp2(arg_refs[slot][qb, hp, hh] - lat[hh]).astype(BF16), v) for hh in range(2)]
        return jnp.where(first, pv[0], pv[1])

    def stage_a_all(jj, w, slot):
        cmin = None
        for qb, hp in groups:
            c = carried(w, slot, qb, hp, scores(jj, w, qb, hp))
            cmin = c if cmin is None else jnp.minimum(cmin, c)
        return jnp.min(cmin)

    def finalize(jj):
        for qb, hp in groups:
            qoff, _ = offsets(jj, 0, qb)
            za = za_ref[pl.ds(qoff, tq), cols_of(hp)].astype(F32)
            o_ref[pl.ds(qoff, tq), cols_of(hp)] = (acc_ref[1, qb, hp] * _silu(za)).astype(o_ref.dtype)

    def next_item(jj, w, cmin):
        more = jnp.logical_and(w < jj, cmin < ATTN_SKIP_LOG2)
        return jnp.where(more, jj, jj + 1), jnp.where(more, w + 1, 0)

    def tick(cj, cw, nj, nw, slot):
        aj = jnp.minimum(nj, njj - 1)
        csel = jnp.minimum(cw, 1)
        pending = (laters(slot, *groups[0]), scores(aj, nw, *groups[0]))
        cmin = None
        for g, (qb, hp) in enumerate(groups):
            lat, zs = pending
            if g + 1 < len(groups):
                pending = (laters(slot, *groups[g + 1]), scores(aj, nw, *groups[g + 1]))
            contrib = weighted(cj, cw, slot, qb, hp, lat)
            c = carried(nw, 1 - slot, qb, hp, zs)
            cmin = c if cmin is None else jnp.minimum(cmin, c)
            acc_ref[1, qb, hp] = acc_ref[csel, qb, hp] + contrib
        cmin = jnp.min(cmin)

        @pl.when(nj != cj)
        def _():
            finalize(cj)

        return next_item(nj, nw, cmin)

    zero = jnp.int32(0)
    nj0, nw0 = next_item(zero, zero, stage_a_all(zero, zero, 0))

    def cond(state):
        return state[0] < njj

    def body(state):
        cj, cw, nj, nw = state
        n2j, n2w = tick(cj, cw, nj, nw, 0)

        def second_tick():
            n3j, n3w = tick(nj, nw, n2j, n2w, 1)
            return n2j, n2w, n3j, n3w

        return lax.cond(nj < njj, second_tick, lambda: (nj, nw, n2j, n2w))

    lax.while_loop(cond, body, (zero, zero, nj0, nw0))


def _sb_attention(proj, batch, seq):
    pairs = ATTN_PAIRS
    width = pairs * LANES
    tq, tk = ATTN_TQ, ATTN_TK
    nqb = tk // tq
    assert seq % tk == 0 and tk % tq == 0 and tq <= LANES

    def col_spec(off):
        return pl.BlockSpec((seq, width), lambda b, p: (b, off // width + p))

    return pl.pallas_call(
        functools.partial(_attn_kernel, seq=seq, pairs=pairs),
        grid=(batch, SB_WIDTH // width),
        in_specs=[col_spec(Q_OFF), col_spec(K_OFF), col_spec(V_OFF), col_spec(ZA_OFF)],
        out_specs=pl.BlockSpec((seq, width), lambda b, p: (b, p)),
        out_shape=jax.ShapeDtypeStruct((batch * seq, SB_WIDTH), BF16),
        scratch_shapes=[
            pltpu.VMEM((tk - tq + seq, width), BF16),
            pltpu.VMEM((tk - tq + seq, width), BF16),
            pltpu.VMEM((2, tq, LANES), F32),
            pltpu.VMEM((nqb, pairs, 2, tq, tk), BF16),
            pltpu.VMEM((nqb, pairs, 2, tq, tk), BF16),
            pltpu.VMEM((nqb, pairs, 2, tq, tk), F32),
            pltpu.VMEM((nqb, pairs, 2, tq, tk), F32),
            pltpu.VMEM((2, nqb, pairs, tq, LANES), F32),
            pltpu.VMEM((2, nqb, pairs, 2, tq, LANES), F32),
        ],
        compiler_params=pltpu.CompilerParams(
            dimension_semantics=("arbitrary", "arbitrary"), vmem_limit_bytes=VMEM_LIMIT),
        name="sb_attn",
    )(proj, proj, proj, proj)


def _ssd_kernel(xbc_ref, zs_ref, dtr_ref, cw_ref, cb_ref, dtb_ref, alog_ref, dsk_ref, nw_ref,
                exp_ref, o_ref, xtail_ref, u_ref, h_ref, y_ref):
    @pl.when(pl.program_id(1) == 0)
    def _():
        xtail_ref[...] = jnp.zeros_like(xtail_ref)
        h_ref[...] = jnp.zeros_like(h_ref)

    for i in range(SSD_STEP_CHUNKS):
        _ssd_chunk(slice(i * SSD_CHUNK, (i + 1) * SSD_CHUNK), xbc_ref, zs_ref, dtr_ref, cw_ref, cb_ref,
                   dtb_ref, alog_ref, dsk_ref, nw_ref, exp_ref, o_ref, xtail_ref, u_ref, h_ref, y_ref)


def _ssd_chunk(rows, xbc_ref, zs_ref, dtr_ref, cw_ref, cb_ref, dtb_ref, alog_ref, dsk_ref, nw_ref,
               exp_ref, o_ref, xtail_ref, u_ref, h_ref, y_ref):
    L = SSD_CHUNK
    tail = SSD_TAIL

    srow = lax.broadcasted_iota(jnp.int32, (L, tail + L), 0)
    scol = lax.broadcasted_iota(jnp.int32, (L, tail + L), 1)
    picks = [jnp.where(scol == srow + (tail - (SSD_CONV - 1 - k)), 1.0, 0.0).astype(BF16)
             for k in range(SSD_CONV - 1)]
    for c0 in range(0, SSD_CONV_DIM, SSD_CONV_STRIP):
        cs = slice(c0, c0 + SSD_CONV_STRIP)
        x_b = xbc_ref[rows, cs]
        xcat = jnp.concatenate([xtail_ref[:, cs], x_b], axis=0)
        acc = cb_ref[:, cs] + cw_ref[SSD_CONV - 1:SSD_CONV, cs] * x_b.astype(F32)
        for k in range(SSD_CONV - 1):
            acc = acc + cw_ref[k:k + 1, cs] * _dot(picks[k], xcat)
        u_ref[:, cs] = _silu(acc)
    xtail_ref[...] = xbc_ref[rows.stop - tail:rows.stop, :]

    dtin = dtr_ref[rows, :] + dtb_ref[...]
    dt = jnp.maximum(dtin, 0.0) + jnp.log(1.0 + jnp.exp(-jnp.abs(dtin)))
    d_a = dt * (-jnp.exp(alog_ref[...]))
    row = lax.broadcasted_iota(jnp.int32, (L, L), 0)
    col = lax.broadcasted_iota(jnp.int32, (L, L), 1)
    causal = row >= col
    ltri = jnp.where(causal, 1.0, 0.0).astype(BF16)
    utri = jnp.where(row <= col, 1.0, 0.0).astype(BF16)
    a_cs = _dot(jnp.concatenate([ltri, ltri], axis=1),
                jnp.concatenate(_split2(d_a), axis=0))
    a_cs_t = _dot(jnp.concatenate(_split2(d_a.T), axis=1),
                  jnp.concatenate([utri, utri], axis=0))

    def stacked(v):
        return jnp.concatenate(_split2(v), axis=1)

    dt_hl = stacked(dt)
    ea_hl = stacked(jnp.exp(a_cs))
    dte_hl = stacked(jnp.exp(a_cs[L - 1:L, :] - a_cs))

    lane = lax.broadcasted_iota(jnp.int32, (1, LANES), 1)
    first = lane < SSD_HEAD_DIM
    heads_per_group = SSD_HEADS // SSD_GROUPS
    for g in range(SSD_GROUPS):
        gs = slice(g * SSD_GROUP_WIDTH, (g + 1) * SSD_GROUP_WIDTH)
        b_off = SSD_WIDTH + g * SSD_STATE
        c_off = SSD_WIDTH + SSD_BC_WIDTH + g * SSD_STATE
        expand = exp_ref[:, gs]
        xs = u_ref[:, gs]
        xdt = xs * _dot(dt_hl, expand)
        xdt_b = xdt.astype(BF16)
        xdte_b = (xdt * _dot(dte_hl, expand)).astype(BF16)
        ea_full = _dot(ea_hl, expand)
        bg = u_ref[:, b_off:b_off + SSD_STATE]
        cg_b = u_ref[:, c_off:c_off + SSD_STATE].astype(BF16)
        cb = _dot_nt(cg_b, bg.astype(BF16))
        states = _dot(bg.T.astype(BF16), xdte_b)
        h_in = h_ref[g]
        y_off = _dot(cg_b, h_in.astype(BF16)) * ea_full
        h_ref[g] = h_in * ea_full[L - 1:L, :] + states
        for j in range(heads_per_group // 2):
            pc = g * SSD_GROUP_WIDTH + j * LANES
            xpair = xdt_b[:, j * LANES:(j + 1) * LANES]
            zero = jnp.zeros_like(xpair)
            xh = (jnp.where(first, xpair, zero), jnp.where(first, zero, xpair))
            ypair = None
            for hh in range(2):
                h = g * heads_per_group + 2 * j + hh
                seg = a_cs[:, h:h + 1] - a_cs_t[h:h + 1, :]
                m = jnp.where(causal, cb * jnp.exp(seg), 0.0).astype(BF16)
                yh = _dot(m, xh[hh])
                ypair = yh if ypair is None else ypair + yh
            y_ref[:, pc:pc + LANES] = ypair
        y = y_ref[:, gs] + y_off + xs * dsk_ref[:, gs]
        y = y * _silu(zs_ref[rows, gs].astype(F32))
        ms = jnp.mean(y * y, axis=-1, keepdims=True)
        o_ref[rows, gs] = (y * lax.rsqrt(ms + EPS) * nw_ref[:, gs]).astype(o_ref.dtype)


def _ssd(proj, dt_raw, conv_w, conv_b, dt_bias, a_log, d_skip, ssm_norm_w, batch, seq):
    L = SSD_CHUNK * SSD_STEP_CHUNKS
    nc = seq // L
    expand = (jnp.arange(SSD_WIDTH)[None, :] // SSD_HEAD_DIM == jnp.arange(LANES)[:, None]).astype(BF16)
    expand = jnp.concatenate([expand, expand], axis=0)

    def pad_heads(v):
        return jnp.pad(v.astype(F32), (0, LANES - SSD_HEADS)).reshape(1, LANES)

    def const(shape):
        return pl.BlockSpec(shape, lambda b, c: (0, 0))

    return pl.pallas_call(
        _ssd_kernel,
        grid=(batch, nc),
        in_specs=[
            pl.BlockSpec((L, SSD_CONV_DIM), lambda b, c: (b * nc + c, XBC_OFF // SSD_CONV_DIM)),
            pl.BlockSpec((L, SSD_WIDTH), lambda b, c: (b * nc + c, ZS_OFF // SSD_WIDTH)),
            pl.BlockSpec((L, LANES), lambda b, c: (b * nc + c, 0)),
            const((SSD_CONV, SSD_CONV_DIM)),
            const((1, SSD_CONV_DIM)),
            const((1, LANES)),
            const((1, LANES)),
            const((1, SSD_WIDTH)),
            const((1, SSD_WIDTH)),
            const((2 * LANES, SSD_WIDTH)),
        ],
        out_specs=pl.BlockSpec((L, SSD_WIDTH), lambda b, c: (b * nc + c, 0)),
        out_shape=jax.ShapeDtypeStruct((batch * seq, SSD_WIDTH), BF16),
        scratch_shapes=[
            pltpu.VMEM((SSD_TAIL, SSD_CONV_DIM), BF16),
            pltpu.VMEM((SSD_CHUNK, SSD_CONV_DIM), F32),
            pltpu.VMEM((SSD_GROUPS, SSD_STATE, SSD_GROUP_WIDTH), F32),
            pltpu.VMEM((SSD_CHUNK, SSD_WIDTH), F32),
        ],
        compiler_params=pltpu.CompilerParams(
            dimension_semantics=("parallel", "arbitrary"), vmem_limit_bytes=VMEM_LIMIT),
        name="ssd",
    )(proj, proj, dt_raw, conv_w.astype(F32), conv_b.astype(F32).reshape(1, -1), pad_heads(dt_bias),
      pad_heads(a_log), jnp.repeat(d_skip.astype(F32), SSD_HEAD_DIM).reshape(1, -1),
      ssm_norm_w.astype(F32).reshape(1, -1), expand)


def _outproj_kernel(ya_ref, yn_ref, ga_ref, gs_ref, x_ref, wa_ref, ws_ref, wo_ref, fw_ref, o_ref,
                    *, final_norm):
    y_a = _dot(ya_ref[...], wa_ref[...])
    y_s = _dot(yn_ref[...], ws_ref[...])
    merged = (jax.nn.sigmoid(ga_ref[...].astype(F32)) * y_a
              + jax.nn.sigmoid(gs_ref[...].astype(F32)) * y_s)
    out = x_ref[...] + _dot(merged.astype(BF16), wo_ref[...])
    if final_norm:
        ms = jnp.mean(out * out, axis=-1, keepdims=True)
        out = (out * lax.rsqrt(ms + EPS)) * fw_ref[...]
    o_ref[...] = out


def _out_proj(ya, yn, proj, x2, w_attn_out, w_ssm_out, w_o, final_w, final_norm):
    t = x2.shape[0]
    tm = min(OUT_TM, t)

    def const(shape):
        return pl.BlockSpec(shape, lambda i: (0, 0))

    return pl.pallas_call(
        functools.partial(_outproj_kernel, final_norm=final_norm),
        grid=(t // tm,),
        in_specs=[
            pl.BlockSpec((tm, SB_WIDTH), lambda i: (i, 0)),
            pl.BlockSpec((tm, SSD_WIDTH), lambda i: (i, 0)),
            pl.BlockSpec((tm, D_MODEL), lambda i: (i, GA_OFF // D_MODEL)),
            pl.BlockSpec((tm, D_MODEL), lambda i: (i, GS_OFF // D_MODEL)),
            pl.BlockSpec((tm, D_MODEL), lambda i: (i, 0)),
            const((SB_WIDTH, D_MODEL)),
            const((SSD_WIDTH, D_MODEL)),
            const((D_MODEL, D_MODEL)),
            const((1, D_MODEL)),
        ],
        out_specs=pl.BlockSpec((tm, D_MODEL), lambda i: (i, 0)),
        out_shape=jax.ShapeDtypeStruct((t, D_MODEL), F32),
        compiler_params=pltpu.CompilerParams(
            dimension_semantics=("parallel",), vmem_limit_bytes=VMEM_LIMIT),
        name="out_proj",
    )(ya, yn, proj, proj, x2, w_attn_out.astype(BF16), w_ssm_out.astype(BF16), w_o.astype(BF16),
      final_w.astype(F32).reshape(1, -1))


def _hybrid_layer(x2, batch, seq, norm_w, w_in, conv_w, conv_b, dt_bias, a_log, d_skip, ssm_norm_w,
                  w_attn_out, w_ssm_out, w_o, final_w, final_norm):
    q_scale = SB_HEAD_DIM ** -0.5 * math.log2(math.e)
    w_t = w_in.T
    w_main_t = jnp.concatenate(
        [w_t[:K_OFF] * q_scale, w_t[K_OFF:DT_SRC_OFF], w_t[GATE_SRC_OFF:]], axis=0).astype(BF16)
    w_dt_t = jnp.pad(w_t[DT_SRC_OFF:GATE_SRC_OFF], ((0, LANES - SSD_HEADS), (0, 0))).astype(BF16)
    proj, dt_raw = _in_proj(x2, norm_w.astype(F32).reshape(1, -1), w_main_t, w_dt_t)
    ya = _sb_attention(proj, batch, seq)
    yn = _ssd(proj, dt_raw, conv_w, conv_b, dt_bias, a_log, d_skip, ssm_norm_w, batch, seq)
    return _out_proj(ya, yn, proj, x2, w_attn_out, w_ssm_out, w_o, final_w, final_norm)


def kernel(x, norm_w, w_in, conv_w, conv_b, dt_bias, a_log, d_skip, ssm_norm_w, w_attn_out, w_ssm_out,
           w_o, final_norm_w):
    batch, seq, _ = x.shape
    depth = norm_w.shape[0]
    h = x.reshape(batch * seq, D_MODEL)
    for layer in range(depth):
        h = _hybrid_layer(h, batch, seq, norm_w[layer], w_in[layer], conv_w[layer], conv_b[layer],
                          dt_bias[layer], a_log[layer], d_skip[layer], ssm_norm_w[layer],
                          w_attn_out[layer], w_ssm_out[layer], w_o[layer], final_norm_w,
                          final_norm=(layer == depth - 1))
    return h.reshape(batch, seq, D_MODEL)
```

```python
import functools
import math

import jax
import jax.numpy as jnp
from jax import lax
from jax.experimental import pallas as pl
from jax.experimental.pallas import tpu as pltpu

F32 = jnp.float32
BF16 = jnp.bfloat16

LANES = 128
D_MODEL = 1024
SB_HEADS = 16
SB_HEAD_DIM = 64
SB_WIDTH = SB_HEADS * SB_HEAD_DIM
SSD_WIDTH = 2 * D_MODEL
SSD_HEAD_DIM = 64
SSD_HEADS = SSD_WIDTH // SSD_HEAD_DIM
SSD_GROUPS = 4
SSD_GROUP_WIDTH = SSD_WIDTH // SSD_GROUPS
SSD_STATE = 128
SSD_CONV = 4
SSD_CHUNK = 128
SSD_STEP_CHUNKS = 2
SSD_CONV_STRIP = 256
SSD_TAIL = 16
SSD_BC_WIDTH = SSD_GROUPS * SSD_STATE
SSD_CONV_DIM = SSD_WIDTH + 2 * SSD_BC_WIDTH
EPS = 1e-6

Q_OFF = 0
K_OFF = Q_OFF + SB_WIDTH
V_OFF = K_OFF + SB_WIDTH
ZA_OFF = V_OFF + SB_WIDTH
ZS_OFF = ZA_OFF + SB_WIDTH
XBC_OFF = ZS_OFF + SSD_WIDTH
GA_OFF = XBC_OFF + SSD_CONV_DIM
GS_OFF = GA_OFF + D_MODEL
N_MAIN = GS_OFF + D_MODEL
DT_SRC_OFF = XBC_OFF + SSD_CONV_DIM
GATE_SRC_OFF = DT_SRC_OFF + SSD_HEADS

VMEM_LIMIT = 48 * 1024 * 1024

IN_TM = 2048
IN_TN = 1024
IN_HEAD_TILES = DT_SRC_OFF // IN_TN
ATTN_TQ = 64
ATTN_TK = 256
ATTN_PAIRS = 4
ATTN_SKIP_LOG2 = 160.0
ATTN_MASK_BIAS = -1e30
OUT_TM = 512


def _dot(a, b):
    return jnp.dot(a, b, preferred_element_type=F32)


def _dot_nt(a, b):
    return lax.dot_general(a, b, (((1,), (1,)), ((), ())), preferred_element_type=F32)


def _split2(v):
    hi = v.astype(BF16)
    lo = (v - hi.astype(F32)).astype(BF16)
    return hi, lo


def _sigmoid(v):
    return 0.5 + 0.5 * jnp.tanh(0.5 * v)


def _silu(v):
    h = 0.5 * v
    return h + h * jnp.tanh(h)


def _inproj_kernel(x_ref, nw_ref, w_ref, wg_ref, wdt_ref, o_ref, dt_ref, h_ref, *, q_scale):
    j = pl.program_id(1)

    @pl.when(j == 0)
    def _():
        x = x_ref[...]
        ms = jnp.mean(x * x, axis=-1, keepdims=True)
        h = ((x * lax.rsqrt(ms + EPS)) * nw_ref[...]).astype(BF16)
        h_ref[...] = h
        dt_ref[...] = _dot_nt(h, wdt_ref[...])

    w = jnp.where(j < IN_HEAD_TILES, w_ref[...], wg_ref[...])
    scale = jnp.where(j == 0, q_scale, 1.0).astype(F32)
    o_ref[...] = (_dot_nt(h_ref[...], w) * scale).astype(o_ref.dtype)


def _in_proj(x2, norm_w, w_all_t, w_gate_t, w_dt_t, q_scale):
    t = x2.shape[0]
    tm = min(IN_TM, t)
    return pl.pallas_call(
        functools.partial(_inproj_kernel, q_scale=q_scale),
        grid=(t // tm, N_MAIN // IN_TN),
        in_specs=[
            pl.BlockSpec((tm, D_MODEL), lambda i, j: (i, 0)),
            pl.BlockSpec((1, D_MODEL), lambda i, j: (0, 0)),
            pl.BlockSpec((IN_TN, D_MODEL), lambda i, j: (jnp.minimum(j, IN_HEAD_TILES - 1), 0)),
            pl.BlockSpec((IN_TN, D_MODEL), lambda i, j: (jnp.maximum(j - IN_HEAD_TILES, 0), 0)),
            pl.BlockSpec((LANES, D_MODEL), lambda i, j: (0, 0)),
        ],
        out_specs=[
            pl.BlockSpec((tm, IN_TN), lambda i, j: (i, j)),
            pl.BlockSpec((tm, LANES), lambda i, j: (i, 0)),
        ],
        out_shape=[
            jax.ShapeDtypeStruct((t, N_MAIN), BF16),
            jax.ShapeDtypeStruct((t, LANES), F32),
        ],
        scratch_shapes=[pltpu.VMEM((tm, D_MODEL), BF16)],
        compiler_params=pltpu.CompilerParams(
            dimension_semantics=("parallel", "arbitrary"), vmem_limit_bytes=VMEM_LIMIT),
        name="in_proj",
    )(x2, norm_w, w_all_t, w_gate_t, w_dt_t)


def _attn_kernel(q_ref, k_ref, v_ref, za_ref, o_ref, kp_ref, vp_ref, bias_ref, sp0_ref, sp1_ref,
                 arg0_ref, arg1_ref, acc_ref, c_ref, *, seq, pairs):
    tq, tk = ATTN_TQ, ATTN_TK
    pad = tk - tq
    nqb = tk // tq
    width = pairs * LANES
    njj = seq // tk
    sp_refs = (sp0_ref, sp1_ref)
    arg_refs = (arg0_ref, arg1_ref)
    sign = jnp.uint32(0x80000000)
    lane = lax.broadcasted_iota(jnp.int32, (1, LANES), 1)
    first = lane < SB_HEAD_DIM

    kp_ref[0:pad, :] = jnp.zeros((pad, width), BF16)
    kp_ref[pad:pad + seq, :] = k_ref[...]
    vp_ref[0:pad, :] = jnp.zeros((pad, width), BF16)
    vp_ref[pad:pad + seq, :] = v_ref[...]

    @pl.when(jnp.logical_and(pl.program_id(0) == 0, pl.program_id(1) == 0))
    def _():
        acc_ref[0] = jnp.zeros(acc_ref.shape[1:], F32)
        c_ref[0] = jnp.zeros(c_ref.shape[1:], F32)


    row = lax.broadcasted_iota(jnp.int32, (tq, LANES), 0)
    col = lax.broadcasted_iota(jnp.int32, (tq, LANES), 1)
    bias_ref[0] = jnp.where(col < row + (LANES - tq), 0.0, ATTN_MASK_BIAS)
    bias_ref[1] = jnp.zeros((tq, LANES), F32)
    trow = lax.broadcasted_iota(jnp.int32, (tk, tk), 0)
    tcol = lax.broadcasted_iota(jnp.int32, (tk, tk), 1)
    tri = jnp.where(trow > tcol, 1.0, 0.0).astype(BF16)

    def offsets(jj, w, qb):
        qoff = pl.multiple_of((nqb * jj + qb) * tq, tq)
        koff = pl.multiple_of((jj - w) * tk + qb * tq, tq)
        return qoff, koff

    groups = [(qb, hp) for qb in range(nqb) for hp in range(pairs)]

    def cols_of(hp):
        return slice(hp * LANES, (hp + 1) * LANES)

    def scores(jj, w, qb, hp):
        qoff, koff = offsets(jj, w, qb)
        k = kp_ref[pl.ds(koff, tk), cols_of(hp)]
        bias = bias_ref[jnp.minimum(w, 1)]
        q = q_ref[pl.ds(qoff, tq), cols_of(hp)]
        zero = jnp.zeros_like(q)
        zs = [_dot_nt(qh, k) for qh in (jnp.where(first, q, zero), jnp.where(first, zero, q))]
        return [jnp.concatenate([z[:, :tk - LANES], z[:, tk - LANES:] + bias], axis=1) for z in zs]

    def carried(w, slot, qb, hp, zs):
        wsel = jnp.minimum(w, 1)
        cmin = None
        for hh in range(2):
            z = zs[hh]
            neg_abs = pltpu.bitcast(pltpu.bitcast(z, jnp.uint32) | sign, F32)
            sp = jnp.maximum(z, 0.0) + jnp.log2(1.0 + jnp.exp2(neg_abs))
            sp_refs[slot][qb, hp, hh] = sp.astype(BF16)
            c_in = c_ref[wsel, qb, hp, hh]
            arg_refs[slot][qb, hp, hh] = (z - sp) - jnp.concatenate([c_in] * (tk // LANES), axis=1)
            total = jnp.broadcast_to(jnp.sum(sp, axis=-1, keepdims=True), (tq, LANES))
            c_new = c_in + total
            c_ref[1, qb, hp, hh] = c_new
            cmin = c_new if cmin is None else jnp.minimum(cmin, c_new)
        return cmin

    def laters(slot, qb, hp):
        return [_dot(sp_refs[slot][qb, hp, hh], tri) for hh in range(2)]

    def weighted(jj, w, slot, qb, hp, lat):
        _, koff = offsets(jj, w, qb)
        v = vp_ref[pl.ds(koff, tk), cols_of(hp)]
        pv = [_dot(jnp.exp2(arg_refs[slot][qb, hp, hh] - lat[hh]).astype(BF16), v) for hh in range(2)]
        return jnp.where(first, pv[0], pv[1])

    def stage_a_all(jj, w, slot):
        cmin = None
        for qb, hp in groups:
            c = carried(w, slot, qb, hp, scores(jj, w, qb, hp))
            cmin = c if cmin is None else jnp.minimum(cmin, c)
        return jnp.min(cmin)

    def finalize(jj):
        for qb, hp in groups:
            qoff, _ = offsets(jj, 0, qb)
            za = za_ref[pl.ds(qoff, tq), cols_of(hp)].astype(F32)
            o_ref[pl.ds(qoff, tq), cols_of(hp)] = (acc_ref[1, qb, hp] * _silu(za)).astype(o_ref.dtype)

    def next_item(jj, w, cmin):
        more = jnp.logical_and(w < jj, cmin < ATTN_SKIP_LOG2)
        return jnp.where(more, jj, jj + 1), jnp.where(more, w + 1, 0)

    def tick(cj, cw, nj, nw, slot):
        aj = jnp.minimum(nj, njj - 1)
        csel = jnp.minimum(cw, 1)
        pending = (laters(slot, *groups[0]), scores(aj, nw, *groups[0]))
        cmin = None
        for g, (qb, hp) in enumerate(groups):
            lat, zs = pending
            if g + 1 < len(groups):
                pending = (laters(slot, *groups[g + 1]), scores(aj, nw, *groups[g + 1]))
            contrib = weighted(cj, cw, slot, qb, hp, lat)
            c = carried(nw, 1 - slot, qb, hp, zs)
            cmin = c if cmin is None else jnp.minimum(cmin, c)
            acc_ref[1, qb, hp] = acc_ref[csel, qb, hp] + contrib
        cmin = jnp.min(cmin)

        @pl.when(nj != cj)
        def _():
            finalize(cj)

        return next_item(nj, nw, cmin)

    zero = jnp.int32(0)
    nj0, nw0 = next_item(zero, zero, stage_a_all(zero, zero, 0))

    def cond(state):
        return state[0] < njj

    def body(state):
        cj, cw, nj, nw = state
        n2j, n2w = tick(cj, cw, nj, nw, 0)

        def second_tick():
            n3j, n3w = tick(nj, nw, n2j, n2w, 1)
            return n2j, n2w, n3j, n3w

        return lax.cond(nj < njj, second_tick, lambda: (nj, nw, n2j, n2w))

    lax.while_loop(cond, body, (zero, zero, nj0, nw0))


def _sb_attention(proj, batch, seq):
    pairs = ATTN_PAIRS
    width = pairs * LANES
    tq, tk = ATTN_TQ, ATTN_TK
    nqb = tk // tq
    assert seq % tk == 0 and tk % tq == 0 and tq <= LANES

    def col_spec(off):
        return pl.BlockSpec((seq, width), lambda b, p: (b, off // width + p))

    return pl.pallas_call(
        functools.partial(_attn_kernel, seq=seq, pairs=pairs),
        grid=(batch, SB_WIDTH // width),
        in_specs=[col_spec(Q_OFF), col_spec(K_OFF), col_spec(V_OFF), col_spec(ZA_OFF)],
        out_specs=pl.BlockSpec((seq, width), lambda b, p: (b, p)),
        out_shape=jax.ShapeDtypeStruct((batch * seq, SB_WIDTH), BF16),
        scratch_shapes=[
            pltpu.VMEM((tk - tq + seq, width), BF16),
            pltpu.VMEM((tk - tq + seq, width), BF16),
            pltpu.VMEM((2, tq, LANES), F32),
            pltpu.VMEM((nqb, pairs, 2, tq, tk), BF16),
            pltpu.VMEM((nqb, pairs, 2, tq, tk), BF16),
            pltpu.VMEM((nqb, pairs, 2, tq, tk), F32),
            pltpu.VMEM((nqb, pairs, 2, tq, tk), F32),
            pltpu.VMEM((2, nqb, pairs, tq, LANES), F32),
            pltpu.VMEM((2, nqb, pairs, 2, tq, LANES), F32),
        ],
        compiler_params=pltpu.CompilerParams(
            dimension_semantics=("arbitrary", "arbitrary"), vmem_limit_bytes=VMEM_LIMIT),
        name="sb_attn",
    )(proj, proj, proj, proj)


def _ssd_kernel(xbc_ref, zs_ref, dtr_ref, cw_ref, cb_ref, dtb_ref, alog_ref, dsk_ref, nw_ref,
                exp_ref, o_ref, xtail_ref, u_ref, h_ref, y_ref):
    @pl.when(pl.program_id(1) == 0)
    def _():
        xtail_ref[...] = jnp.zeros_like(xtail_ref)
        h_ref[...] = jnp.zeros_like(h_ref)

    for i in range(SSD_STEP_CHUNKS):
        _ssd_chunk(slice(i * SSD_CHUNK, (i + 1) * SSD_CHUNK), xbc_ref, zs_ref, dtr_ref, cw_ref, cb_ref,
                   dtb_ref, alog_ref, dsk_ref, nw_ref, exp_ref, o_ref, xtail_ref, u_ref, h_ref, y_ref)


def _ssd_chunk(rows, xbc_ref, zs_ref, dtr_ref, cw_ref, cb_ref, dtb_ref, alog_ref, dsk_ref, nw_ref,
               exp_ref, o_ref, xtail_ref, u_ref, h_ref, y_ref):
    L = SSD_CHUNK
    tail = SSD_TAIL

    srow = lax.broadcasted_iota(jnp.int32, (L, tail + L), 0)
    scol = lax.broadcasted_iota(jnp.int32, (L, tail + L), 1)
    picks = [jnp.where(scol == srow + (tail - (SSD_CONV - 1 - k)), 1.0, 0.0).astype(BF16)
             for k in range(SSD_CONV - 1)]
    for c0 in range(0, SSD_CONV_DIM, SSD_CONV_STRIP):
        cs = slice(c0, c0 + SSD_CONV_STRIP)
        x_b = xbc_ref[rows, cs]
        xcat = jnp.concatenate([xtail_ref[:, cs], x_b], axis=0)
        acc = cb_ref[:, cs] + cw_ref[SSD_CONV - 1:SSD_CONV, cs] * x_b.astype(F32)
        for k in range(SSD_CONV - 1):
            acc = acc + cw_ref[k:k + 1, cs] * _dot(picks[k], xcat)
        u_ref[:, cs] = _silu(acc)
    xtail_ref[...] = xbc_ref[rows.stop - tail:rows.stop, :]

    dtin = dtr_ref[rows, :] + dtb_ref[...]
    dt = jnp.maximum(dtin, 0.0) + jnp.log(1.0 + jnp.exp(-jnp.abs(dtin)))
    d_a = dt * (-jnp.exp(alog_ref[...]))
    row = lax.broadcasted_iota(jnp.int32, (L, L), 0)
    col = lax.broadcasted_iota(jnp.int32, (L, L), 1)
    causal = row >= col
    ltri = jnp.where(causal, 1.0, 0.0).astype(BF16)
    utri = jnp.where(row <= col, 1.0, 0.0).astype(BF16)
    a_cs = _dot(jnp.concatenate([ltri, ltri], axis=1),
                jnp.concatenate(_split2(d_a), axis=0))
    a_cs_t = _dot(jnp.concatenate(_split2(d_a.T), axis=1),
                  jnp.concatenate([utri, utri], axis=0))

    def stacked(v):
        return jnp.concatenate(_split2(v), axis=1)

    dt_hl = stacked(dt)
    ea_hl = stacked(jnp.exp(a_cs))
    dte_hl = stacked(jnp.exp(a_cs[L - 1:L, :] - a_cs))

    lane = lax.broadcasted_iota(jnp.int32, (1, LANES), 1)
    first = lane < SSD_HEAD_DIM
    heads_per_group = SSD_HEADS // SSD_GROUPS
    for g in range(SSD_GROUPS):
        gs = slice(g * SSD_GROUP_WIDTH, (g + 1) * SSD_GROUP_WIDTH)
        b_off = SSD_WIDTH + g * SSD_STATE
        c_off = SSD_WIDTH + SSD_BC_WIDTH + g * SSD_STATE
        expand = exp_ref[:, gs]
        xs = u_ref[:, gs]
        xdt = xs * _dot(dt_hl, expand)
        xdt_b = xdt.astype(BF16)
        xdte_b = (xdt * _dot(dte_hl, expand)).astype(BF16)
        ea_full = _dot(ea_hl, expand)
        bg = u_ref[:, b_off:b_off + SSD_STATE]
        cg_b = u_ref[:, c_off:c_off + SSD_STATE].astype(BF16)
        cb = _dot_nt(cg_b, bg.astype(BF16))
        states = _dot(bg.T.astype(BF16), xdte_b)
        h_in = h_ref[g]
        y_off = _dot(cg_b, h_in.astype(BF16)) * ea_full
        h_ref[g] = h_in * ea_full[L - 1:L, :] + states
        for j in range(heads_per_group // 2):
            pc = g * SSD_GROUP_WIDTH + j * LANES
            xpair = xdt_b[:, j * LANES:(j + 1) * LANES]
            zero = jnp.zeros_like(xpair)
            xh = (jnp.where(first, xpair, zero), jnp.where(first, zero, xpair))
            ypair = None
            for hh in range(2):
                h = g * heads_per_group + 2 * j + hh
                seg = a_cs[:, h:h + 1] - a_cs_t[h:h + 1, :]
                m = jnp.where(causal, cb * jnp.exp(seg), 0.0).astype(BF16)
                yh = _dot(m, xh[hh])
                ypair = yh if ypair is None else ypair + yh
            y_ref[:, pc:pc + LANES] = ypair
        y = y_ref[:, gs] + y_off + xs * dsk_ref[:, gs]
        y = y * _silu(zs_ref[rows, gs].astype(F32))
        ms = jnp.mean(y * y, axis=-1, keepdims=True)
        o_ref[rows, gs] = (y * lax.rsqrt(ms + EPS) * nw_ref[:, gs]).astype(o_ref.dtype)


def _ssd(proj, dt_raw, conv_w, conv_b, dt_bias, a_log, d_skip, ssm_norm_w, batch, seq):
    L = SSD_CHUNK * SSD_STEP_CHUNKS
    nc = seq // L
    expand = (jnp.arange(SSD_WIDTH)[None, :] // SSD_HEAD_DIM == jnp.arange(LANES)[:, None]).astype(BF16)
    expand = jnp.concatenate([expand, expand], axis=0)

    def pad_heads(v):
        return jnp.pad(v.astype(F32), (0, LANES - SSD_HEADS)).reshape(1, LANES)

    def const(shape):
        return pl.BlockSpec(shape, lambda b, c: (0, 0))

    return pl.pallas_call(
        _ssd_kernel,
        grid=(batch, nc),
        in_specs=[
            pl.BlockSpec((L, SSD_CONV_DIM), lambda b, c: (b * nc + c, XBC_OFF // SSD_CONV_DIM)),
            pl.BlockSpec((L, SSD_WIDTH), lambda b, c: (b * nc + c, ZS_OFF // SSD_WIDTH)),
            pl.BlockSpec((L, LANES), lambda b, c: (b * nc + c, 0)),
            const((SSD_CONV, SSD_CONV_DIM)),
            const((1, SSD_CONV_DIM)),
            const((1, LANES)),
            const((1, LANES)),
            const((1, SSD_WIDTH)),
            const((1, SSD_WIDTH)),
            const((2 * LANES, SSD_WIDTH)),
        ],
        out_specs=pl.BlockSpec((L, SSD_WIDTH), lambda b, c: (b * nc + c, 0)),
        out_shape=jax.ShapeDtypeStruct((batch * seq, SSD_WIDTH), BF16),
        scratch_shapes=[
            pltpu.VMEM((SSD_TAIL, SSD_CONV_DIM), BF16),
            pltpu.VMEM((SSD_CHUNK, SSD_CONV_DIM), F32),
            pltpu.VMEM((SSD_GROUPS, SSD_STATE, SSD_GROUP_WIDTH), F32),
            pltpu.VMEM((SSD_CHUNK, SSD_WIDTH), F32),
        ],
        compiler_params=pltpu.CompilerParams(
            dimension_semantics=("parallel", "arbitrary"), vmem_limit_bytes=VMEM_LIMIT),
        name="ssd",
    )(proj, proj, dt_raw, conv_w.astype(F32), conv_b.astype(F32).reshape(1, -1), pad_heads(dt_bias),
      pad_heads(a_log), jnp.repeat(d_skip.astype(F32), SSD_HEAD_DIM).reshape(1, -1),
      ssm_norm_w.astype(F32).reshape(1, -1), expand)


def _outproj_kernel(ya_ref, yn_ref, ga_ref, gs_ref, x_ref, wa_ref, ws_ref, wo_ref, fw_ref, o_ref,
                    *, final_norm):
    y_a = _dot(ya_ref[...], wa_ref[...])
    y_s = _dot(yn_ref[...], ws_ref[...])
    merged = _sigmoid(ga_ref[...].astype(F32)) * y_a + _sigmoid(gs_ref[...].astype(F32)) * y_s
    out = x_ref[...] + _dot(merged.astype(BF16), wo_ref[...])
    if final_norm:
        ms = jnp.mean(out * out, axis=-1, keepdims=True)
        out = (out * lax.rsqrt(ms + EPS)) * fw_ref[...]
    o_ref[...] = out


def _out_proj(ya, yn, proj, x2, w_attn_out, w_ssm_out, w_o, final_w, final_norm):
    t = x2.shape[0]
    tm = min(OUT_TM, t)

    def const(shape):
        return pl.BlockSpec(shape, lambda i: (0, 0))

    return pl.pallas_call(
        functools.partial(_outproj_kernel, final_norm=final_norm),
        grid=(t // tm,),
        in_specs=[
            pl.BlockSpec((tm, SB_WIDTH), lambda i: (i, 0)),
            pl.BlockSpec((tm, SSD_WIDTH), lambda i: (i, 0)),
            pl.BlockSpec((tm, D_MODEL), lambda i: (i, GA_OFF // D_MODEL)),
            pl.BlockSpec((tm, D_MODEL), lambda i: (i, GS_OFF // D_MODEL)),
            pl.BlockSpec((tm, D_MODEL), lambda i: (i, 0)),
            const((SB_WIDTH, D_MODEL)),
            const((SSD_WIDTH, D_MODEL)),
            const((D_MODEL, D_MODEL)),
            const((1, D_MODEL)),
        ],
        out_specs=pl.BlockSpec((tm, D_MODEL), lambda i: (i, 0)),
        out_shape=jax.ShapeDtypeStruct((t, D_MODEL), F32),
        compiler_params=pltpu.CompilerParams(
            dimension_semantics=("parallel",), vmem_limit_bytes=VMEM_LIMIT),
        name="out_proj",
    )(ya, yn, proj, proj, x2, w_attn_out.astype(BF16), w_ssm_out.astype(BF16), w_o.astype(BF16),
      final_w.astype(F32).reshape(1, -1))


def _hybrid_layer(x2, batch, seq, norm_w, w_in, conv_w, conv_b, dt_bias, a_log, d_skip, ssm_norm_w,
                  w_attn_out, w_ssm_out, w_o, final_w, final_norm):
    q_scale = SB_HEAD_DIM ** -0.5 * math.log2(math.e)
    w_all_t = w_in.T.astype(BF16)
    w_gate_t = w_all_t[GATE_SRC_OFF:]
    w_dt_t = jnp.pad(w_all_t[DT_SRC_OFF:GATE_SRC_OFF], ((0, LANES - SSD_HEADS), (0, 0)))
    proj, dt_raw = _in_proj(x2, norm_w.astype(F32).reshape(1, -1), w_all_t, w_gate_t, w_dt_t, q_scale)
    ya = _sb_attention(proj, batch, seq)
    yn = _ssd(proj, dt_raw, conv_w, conv_b, dt_bias, a_log, d_skip, ssm_norm_w, batch, seq)
    return _out_proj(ya, yn, proj, x2, w_attn_out, w_ssm_out, w_o, final_w, final_norm)


def kernel(x, norm_w, w_in, conv_w, conv_b, dt_bias, a_log, d_skip, ssm_norm_w, w_attn_out, w_ssm_out,
           w_o, final_norm_w):
    batch, seq, _ = x.shape
    depth = norm_w.shape[0]
    h = x.reshape(batch * seq, D_MODEL)
    for layer in range(depth):
        h = _hybrid_layer(h, batch, seq, norm_w[layer], w_in[layer], conv_w[layer], conv_b[layer],
                          dt_bias[layer], a_log[layer], d_skip[layer], ssm_norm_w[layer],
                          w_attn_out[layer], w_ssm_out[layer], w_o[layer], final_norm_w,
                          final_norm=(layer == depth - 1))
    return h.reshape(batch, seq, D_MODEL)
```

```python
import functools
import math

import jax
import jax.numpy as jnp
from jax import lax
from jax.experimental import pallas as pl
from jax.experimental.pallas import tpu as pltpu

F32 = jnp.float32
BF16 = jnp.bfloat16

LANES = 128
D_MODEL = 1024
SB_HEADS = 16
SB_HEAD_DIM = 64
SB_WIDTH = SB_HEADS * SB_HEAD_DIM
SSD_WIDTH = 2 * D_MODEL
SSD_HEAD_DIM = 64
SSD_HEADS = SSD_WIDTH // SSD_HEAD_DIM
SSD_GROUPS = 4
SSD_GROUP_WIDTH = SSD_WIDTH // SSD_GROUPS
SSD_STATE = 128
SSD_CONV = 4
SSD_CHUNK = 128
SSD_STEP_CHUNKS = 4
SSD_CONV_STRIP = 256
SSD_TAIL = 16
SSD_BC_WIDTH = SSD_GROUPS * SSD_STATE
SSD_CONV_DIM = SSD_WIDTH + 2 * SSD_BC_WIDTH
EPS = 1e-6

Q_OFF = 0
K_OFF = Q_OFF + SB_WIDTH
V_OFF = K_OFF + SB_WIDTH
ZA_OFF = V_OFF + SB_WIDTH
ZS_OFF = ZA_OFF + SB_WIDTH
XBC_OFF = ZS_OFF + SSD_WIDTH
GA_OFF = XBC_OFF + SSD_CONV_DIM
GS_OFF = GA_OFF + D_MODEL
N_MAIN = GS_OFF + D_MODEL
DT_SRC_OFF = XBC_OFF + SSD_CONV_DIM
GATE_SRC_OFF = DT_SRC_OFF + SSD_HEADS

VMEM_LIMIT = 48 * 1024 * 1024

IN_TM = 2048
IN_TN = 1024
IN_HEAD_TILES = DT_SRC_OFF // IN_TN
ATTN_TQ = 64
ATTN_TK = 256
ATTN_PAIRS = 4
ATTN_SKIP_LOG2 = 160.0
ATTN_MASK_BIAS = -1e30
OUT_TM = 512


def _dot(a, b):
    return jnp.dot(a, b, preferred_element_type=F32)


def _dot_nt(a, b):
    return lax.dot_general(a, b, (((1,), (1,)), ((), ())), preferred_element_type=F32)


def _split2(v):
    hi = v.astype(BF16)
    lo = (v - hi.astype(F32)).astype(BF16)
    return hi, lo


def _sigmoid(v):
    return 0.5 + 0.5 * jnp.tanh(0.5 * v)


def _silu(v):
    h = 0.5 * v
    return h + h * jnp.tanh(h)


def _inproj_kernel(x_ref, nw_ref, w_ref, wg_ref, wdt_ref, o_ref, dt_ref, h_ref, *, q_scale):
    j = pl.program_id(1)

    @pl.when(j == 0)
    def _():
        x = x_ref[...]
        ms = jnp.mean(x * x, axis=-1, keepdims=True)
        h = ((x * lax.rsqrt(ms + EPS)) * nw_ref[...]).astype(BF16)
        h_ref[...] = h
        dt_ref[...] = _dot_nt(h, wdt_ref[...])

    w = jnp.where(j < IN_HEAD_TILES, w_ref[...], wg_ref[...])
    scale = jnp.where(j == 0, q_scale, 1.0).astype(F32)
    o_ref[...] = (_dot_nt(h_ref[...], w) * scale).astype(o_ref.dtype)


def _in_proj(x2, norm_w, w_all_t, w_gate_t, w_dt_t, q_scale):
    t = x2.shape[0]
    tm = min(IN_TM, t)
    return pl.pallas_call(
        functools.partial(_inproj_kernel, q_scale=q_scale),
        grid=(t // tm, N_MAIN // IN_TN),
        in_specs=[
            pl.BlockSpec((tm, D_MODEL), lambda i, j: (i, 0)),
            pl.BlockSpec((1, D_MODEL), lambda i, j: (0, 0)),
            pl.BlockSpec((IN_TN, D_MODEL), lambda i, j: (jnp.minimum(j, IN_HEAD_TILES - 1), 0)),
            pl.BlockSpec((IN_TN, D_MODEL), lambda i, j: (jnp.maximum(j - IN_HEAD_TILES, 0), 0)),
            pl.BlockSpec((LANES, D_MODEL), lambda i, j: (0, 0)),
        ],
        out_specs=[
            pl.BlockSpec((tm, IN_TN), lambda i, j: (i, j)),
            pl.BlockSpec((tm, LANES), lambda i, j: (i, 0)),
        ],
        out_shape=[
            jax.ShapeDtypeStruct((t, N_MAIN), BF16),
            jax.ShapeDtypeStruct((t, LANES), F32),
        ],
        scratch_shapes=[pltpu.VMEM((tm, D_MODEL), BF16)],
        compiler_params=pltpu.CompilerParams(
            dimension_semantics=("parallel", "arbitrary"), vmem_limit_bytes=VMEM_LIMIT),
        name="in_proj",
    )(x2, norm_w, w_all_t, w_gate_t, w_dt_t)


def _attn_kernel(q_ref, k_ref, v_ref, za_ref, o_ref, kp_ref, vp_ref, bias_ref, sp0_ref, sp1_ref,
                 arg0_ref, arg1_ref, acc_ref, c_ref, *, seq, pairs):
    tq, tk = ATTN_TQ, ATTN_TK
    pad = tk - tq
    nqb = tk // tq
    width = pairs * LANES
    njj = seq // tk
    sp_refs = (sp0_ref, sp1_ref)
    arg_refs = (arg0_ref, arg1_ref)
    sign = jnp.uint32(0x80000000)
    lane = lax.broadcasted_iota(jnp.int32, (1, LANES), 1)
    first = lane < SB_HEAD_DIM

    kp_ref[0:pad, :] = jnp.zeros((pad, width), BF16)
    kp_ref[pad:pad + seq, :] = k_ref[...]
    vp_ref[0:pad, :] = jnp.zeros((pad, width), BF16)
    vp_ref[pad:pad + seq, :] = v_ref[...]

    @pl.when(jnp.logical_and(pl.program_id(0) == 0, pl.program_id(1) == 0))
    def _():
        acc_ref[0] = jnp.zeros(acc_ref.shape[1:], F32)
        c_ref[0] = jnp.zeros(c_ref.shape[1:], F32)


    row = lax.broadcasted_iota(jnp.int32, (tq, LANES), 0)
    col = lax.broadcasted_iota(jnp.int32, (tq, LANES), 1)
    bias_ref[0] = jnp.where(col < row + (LANES - tq), 0.0, ATTN_MASK_BIAS)
    bias_ref[1] = jnp.zeros((tq, LANES), F32)
    trow = lax.broadcasted_iota(jnp.int32, (tk, tk), 0)
    tcol = lax.broadcasted_iota(jnp.int32, (tk, tk), 1)
    tri = jnp.where(trow > tcol, 1.0, 0.0).astype(BF16)

    def offsets(jj, w, qb):
        qoff = pl.multiple_of((nqb * jj + qb) * tq, tq)
        koff = pl.multiple_of((jj - w) * tk + qb * tq, tq)
        return qoff, koff

    groups = [(qb, hp) for qb in range(nqb) for hp in range(pairs)]

    def cols_of(hp):
        return slice(hp * LANES, (hp + 1) * LANES)

    def scores(jj, w, qb, hp):
        qoff, koff = offsets(jj, w, qb)
        k = kp_ref[pl.ds(koff, tk), cols_of(hp)]
        bias = bias_ref[jnp.minimum(w, 1)]
        q = q_ref[pl.ds(qoff, tq), cols_of(hp)]
        zero = jnp.zeros_like(q)
        zs = [_dot_nt(qh, k) for qh in (jnp.where(first, q, zero), jnp.where(first, zero, q))]
        return [jnp.concatenate([z[:, :tk - LANES], z[:, tk - LANES:] + bias], axis=1) for z in zs]

    def carried(w, slot, qb, hp, zs):
        wsel = jnp.minimum(w, 1)
        cmin = None
        for hh in range(2):
            z = zs[hh]
            neg_abs = pltpu.bitcast(pltpu.bitcast(z, jnp.uint32) | sign, F32)
            sp = jnp.maximum(z, 0.0) + jnp.log2(1.0 + jnp.exp2(neg_abs))
            sp_refs[slot][qb, hp, hh] = sp.astype(BF16)
            c_in = c_ref[wsel, qb, hp, hh]
            arg_refs[slot][qb, hp, hh] = (z - sp) - jnp.concatenate([c_in] * (tk // LANES), axis=1)
            total = jnp.broadcast_to(jnp.sum(sp, axis=-1, keepdims=True), (tq, LANES))
            c_new = c_in + total
            c_ref[1, qb, hp, hh] = c_new
            cmin = c_new if cmin is None else jnp.minimum(cmin, c_new)
        return cmin

    def laters(slot, qb, hp):
        return [_dot(sp_refs[slot][qb, hp, hh], tri) for hh in range(2)]

    def weighted(jj, w, slot, qb, hp, lat):
        _, koff = offsets(jj, w, qb)
        v = vp_ref[pl.ds(koff, tk), cols_of(hp)]
        pv = [_dot(jnp.exp2(arg_refs[slot][qb, hp, hh] - lat[hh]).astype(BF16), v) for hh in range(2)]
        return jnp.where(first, pv[0], pv[1])

    def stage_a_all(jj, w, slot):
        cmin = None
        for qb, hp in groups:
            c = carried(w, slot, qb, hp, scores(jj, w, qb, hp))
            cmin = c if cmin is None else jnp.minimum(cmin, c)
        return jnp.min(cmin)

    def finalize(jj):
        for qb, hp in groups:
            qoff, _ = offsets(jj, 0, qb)
            za = za_ref[pl.ds(qoff, tq), cols_of(hp)].astype(F32)
            o_ref[pl.ds(qoff, tq), cols_of(hp)] = (acc_ref[1, qb, hp] * _silu(za)).astype(o_ref.dtype)

    def next_item(jj, w, cmin):
        more = jnp.logical_and(w < jj, cmin < ATTN_SKIP_LOG2)
        return jnp.where(more, jj, jj + 1), jnp.where(more, w + 1, 0)

    def tick(cj, cw, nj, nw, slot):
        aj = jnp.minimum(nj, njj - 1)
        csel = jnp.minimum(cw, 1)
        pending = (laters(slot, *groups[0]), scores(aj, nw, *groups[0]))
        cmin = None
        for g, (qb, hp) in enumerate(groups):
            lat, zs = pending
            if g + 1 < len(groups):
                pending = (laters(slot, *groups[g + 1]), scores(aj, nw, *groups[g + 1]))
            contrib = weighted(cj, cw, slot, qb, hp, lat)
            c = carried(nw, 1 - slot, qb, hp, zs)
            cmin = c if cmin is None else jnp.minimum(cmin, c)
            acc_ref[1, qb, hp] = acc_ref[csel, qb, hp] + contrib
        cmin = jnp.min(cmin)

        @pl.when(nj != cj)
        def _():
            finalize(cj)

        return next_item(nj, nw, cmin)

    zero = jnp.int32(0)
    nj0, nw0 = next_item(zero, zero, stage_a_all(zero, zero, 0))

    def cond(state):
        return state[0] < njj

    def body(state):
        cj, cw, nj, nw = state
        n2j, n2w = tick(cj, cw, nj, nw, 0)

        def second_tick():
            n3j, n3w = tick(nj, nw, n2j, n2w, 1)
            return n2j, n2w, n3j, n3w

        return lax.cond(nj < njj, second_tick, lambda: (nj, nw, n2j, n2w))

    lax.while_loop(cond, body, (zero, zero, nj0, nw0))


def _sb_attention(proj, batch, seq):
    pairs = ATTN_PAIRS
    width = pairs * LANES
    tq, tk = ATTN_TQ, ATTN_TK
    nqb = tk // tq
    assert seq % tk == 0 and tk % tq == 0 and tq <= LANES

    def col_spec(off):
        return pl.BlockSpec((seq, width), lambda b, p: (b, off // width + p))

    return pl.pallas_call(
        functools.partial(_attn_kernel, seq=seq, pairs=pairs),
        grid=(batch, SB_WIDTH // width),
        in_specs=[col_spec(Q_OFF), col_spec(K_OFF), col_spec(V_OFF), col_spec(ZA_OFF)],
        out_specs=pl.BlockSpec((seq, width), lambda b, p: (b, p)),
        out_shape=jax.ShapeDtypeStruct((batch * seq, SB_WIDTH), BF16),
        scratch_shapes=[
            pltpu.VMEM((tk - tq + seq, width), BF16),
            pltpu.VMEM((tk - tq + seq, width), BF16),
            pltpu.VMEM((2, tq, LANES), F32),
            pltpu.VMEM((nqb, pairs, 2, tq, tk), BF16),
            pltpu.VMEM((nqb, pairs, 2, tq, tk), BF16),
            pltpu.VMEM((nqb, pairs, 2, tq, tk), F32),
            pltpu.VMEM((nqb, pairs, 2, tq, tk), F32),
            pltpu.VMEM((2, nqb, pairs, tq, LANES), F32),
            pltpu.VMEM((2, nqb, pairs, 2, tq, LANES), F32),
        ],
        compiler_params=pltpu.CompilerParams(
            dimension_semantics=("arbitrary", "arbitrary"), vmem_limit_bytes=VMEM_LIMIT),
        name="sb_attn",
    )(proj, proj, proj, proj)


def _ssd_kernel(xbc_ref, zs_ref, dtr_ref, cw_ref, cb_ref, dtb_ref, alog_ref, dsk_ref, nw_ref,
                exp_ref, o_ref, xtail_ref, u_ref, h_ref, y_ref):
    @pl.when(pl.program_id(1) == 0)
    def _():
        xtail_ref[...] = jnp.zeros_like(xtail_ref)
        h_ref[...] = jnp.zeros_like(h_ref)

    for i in range(SSD_STEP_CHUNKS):
        _ssd_chunk(slice(i * SSD_CHUNK, (i + 1) * SSD_CHUNK), xbc_ref, zs_ref, dtr_ref, cw_ref, cb_ref,
                   dtb_ref, alog_ref, dsk_ref, nw_ref, exp_ref, o_ref, xtail_ref, u_ref, h_ref, y_ref)


def _ssd_chunk(rows, xbc_ref, zs_ref, dtr_ref, cw_ref, cb_ref, dtb_ref, alog_ref, dsk_ref, nw_ref,
               exp_ref, o_ref, xtail_ref, u_ref, h_ref, y_ref):
    L = SSD_CHUNK
    tail = SSD_TAIL

    srow = lax.broadcasted_iota(jnp.int32, (L, tail + L), 0)
    scol = lax.broadcasted_iota(jnp.int32, (L, tail + L), 1)
    picks = [jnp.where(scol == srow + (tail - (SSD_CONV - 1 - k)), 1.0, 0.0).astype(BF16)
             for k in range(SSD_CONV - 1)]
    for c0 in range(0, SSD_CONV_DIM, SSD_CONV_STRIP):
        cs = slice(c0, c0 + SSD_CONV_STRIP)
        x_b = xbc_ref[rows, cs]
        xcat = jnp.concatenate([xtail_ref[:, cs], x_b], axis=0)
        acc = cb_ref[:, cs] + cw_ref[SSD_CONV - 1:SSD_CONV, cs] * x_b.astype(F32)
        for k in range(SSD_CONV - 1):
            acc = acc + cw_ref[k:k + 1, cs] * _dot(picks[k], xcat)
        u_ref[:, cs] = _silu(acc)
    xtail_ref[...] = xbc_ref[rows.stop - tail:rows.stop, :]

    dtin = dtr_ref[rows, :] + dtb_ref[...]
    dt = jnp.maximum(dtin, 0.0) + jnp.log(1.0 + jnp.exp(-jnp.abs(dtin)))
    d_a = dt * (-jnp.exp(alog_ref[...]))
    row = lax.broadcasted_iota(jnp.int32, (L, L), 0)
    col = lax.broadcasted_iota(jnp.int32, (L, L), 1)
    causal = row >= col
    ltri = jnp.where(causal, 1.0, 0.0).astype(BF16)
    utri = jnp.where(row <= col, 1.0, 0.0).astype(BF16)
    a_cs = _dot(jnp.concatenate([ltri, ltri], axis=1),
                jnp.concatenate(_split2(d_a), axis=0))
    a_cs_t = _dot(jnp.concatenate(_split2(d_a.T), axis=1),
                  jnp.concatenate([utri, utri], axis=0))

    def stacked(v):
        return jnp.concatenate(_split2(v), axis=1)

    dt_hl = stacked(dt)
    ea_hl = stacked(jnp.exp(a_cs))
    dte_hl = stacked(jnp.exp(a_cs[L - 1:L, :] - a_cs))

    lane = lax.broadcasted_iota(jnp.int32, (1, LANES), 1)
    first = lane < SSD_HEAD_DIM
    heads_per_group = SSD_HEADS // SSD_GROUPS
    for g in range(SSD_GROUPS):
        gs = slice(g * SSD_GROUP_WIDTH, (g + 1) * SSD_GROUP_WIDTH)
        b_off = SSD_WIDTH + g * SSD_STATE
        c_off = SSD_WIDTH + SSD_BC_WIDTH + g * SSD_STATE
        expand = exp_ref[:, gs]
        xs = u_ref[:, gs]
        xdt = xs * _dot(dt_hl, expand)
        xdt_b = xdt.astype(BF16)
        xdte_b = (xdt * _dot(dte_hl, expand)).astype(BF16)
        ea_full = _dot(ea_hl, expand)
        bg = u_ref[:, b_off:b_off + SSD_STATE]
        cg_b = u_ref[:, c_off:c_off + SSD_STATE].astype(BF16)
        cb = _dot_nt(cg_b, bg.astype(BF16))
        states = _dot(bg.T.astype(BF16), xdte_b)
        h_in = h_ref[g]
        y_off = _dot(cg_b, h_in.astype(BF16)) * ea_full
        h_ref[g] = h_in * ea_full[L - 1:L, :] + states
        for j in range(heads_per_group // 2):
            pc = g * SSD_GROUP_WIDTH + j * LANES
            xpair = xdt_b[:, j * LANES:(j + 1) * LANES]
            zero = jnp.zeros_like(xpair)
            xh = (jnp.where(first, xpair, zero), jnp.where(first, zero, xpair))
            ypair = None
            for hh in range(2):
                h = g * heads_per_group + 2 * j + hh
                seg = a_cs[:, h:h + 1] - a_cs_t[h:h + 1, :]
                m = jnp.where(causal, cb * jnp.exp(seg), 0.0).astype(BF16)
                yh = _dot(m, xh[hh])
                ypair = yh if ypair is None else ypair + yh
            y_ref[:, pc:pc + LANES] = ypair
        y = y_ref[:, gs] + y_off + xs * dsk_ref[:, gs]
        y = y * _silu(zs_ref[rows, gs].astype(F32))
        ms = jnp.mean(y * y, axis=-1, keepdims=True)
        o_ref[rows, gs] = (y * lax.rsqrt(ms + EPS) * nw_ref[:, gs]).astype(o_ref.dtype)


def _ssd(proj, dt_raw, conv_w, conv_b, dt_bias, a_log, d_skip, ssm_norm_w, batch, seq):
    L = SSD_CHUNK * SSD_STEP_CHUNKS
    nc = seq // L
    expand = (jnp.arange(SSD_WIDTH)[None, :] // SSD_HEAD_DIM == jnp.arange(LANES)[:, None]).astype(BF16)
    expand = jnp.concatenate([expand, expand], axis=0)

    def pad_heads(v):
        return jnp.pad(v.astype(F32), (0, LANES - SSD_HEADS)).reshape(1, LANES)

    def const(shape):
        return pl.BlockSpec(shape, lambda b, c: (0, 0))

    return pl.pallas_call(
        _ssd_kernel,
        grid=(batch, nc),
        in_specs=[
            pl.BlockSpec((L, SSD_CONV_DIM), lambda b, c: (b * nc + c, XBC_OFF // SSD_CONV_DIM)),
            pl.BlockSpec((L, SSD_WIDTH), lambda b, c: (b * nc + c, ZS_OFF // SSD_WIDTH)),
            pl.BlockSpec((L, LANES), lambda b, c: (b * nc + c, 0)),
            const((SSD_CONV, SSD_CONV_DIM)),
            const((1, SSD_CONV_DIM)),
            const((1, LANES)),
            const((1, LANES)),
            const((1, SSD_WIDTH)),
            const((1, SSD_WIDTH)),
            const((2 * LANES, SSD_WIDTH)),
        ],
        out_specs=pl.BlockSpec((L, SSD_WIDTH), lambda b, c: (b * nc + c, 0)),
        out_shape=jax.ShapeDtypeStruct((batch * seq, SSD_WIDTH), BF16),
        scratch_shapes=[
            pltpu.VMEM((SSD_TAIL, SSD_CONV_DIM), BF16),
            pltpu.VMEM((SSD_CHUNK, SSD_CONV_DIM), F32),
            pltpu.VMEM((SSD_GROUPS, SSD_STATE, SSD_GROUP_WIDTH), F32),
            pltpu.VMEM((SSD_CHUNK, SSD_WIDTH), F32),
        ],
        compiler_params=pltpu.CompilerParams(
            dimension_semantics=("parallel", "arbitrary"), vmem_limit_bytes=VMEM_LIMIT),
        name="ssd",
    )(proj, proj, dt_raw, conv_w.astype(F32), conv_b.astype(F32).reshape(1, -1), pad_heads(dt_bias),
      pad_heads(a_log), jnp.repeat(d_skip.astype(F32), SSD_HEAD_DIM).reshape(1, -1),
      ssm_norm_w.astype(F32).reshape(1, -1), expand)


def _outproj_kernel(ya_ref, yn_ref, ga_ref, gs_ref, x_ref, wa_ref, ws_ref, wo_ref, fw_ref, o_ref,
                    *, final_norm):
    y_a = _dot(ya_ref[...], wa_ref[...])
    y_s = _dot(yn_ref[...], ws_ref[...])
    merged = _sigmoid(ga_ref[...].astype(F32)) * y_a + _sigmoid(gs_ref[...].astype(F32)) * y_s
    out = x_ref[...] + _dot(merged.astype(BF16), wo_ref[...])
    if final_norm:
        ms = jnp.mean(out * out, axis=-1, keepdims=True)
        out = (out * lax.rsqrt(ms + EPS)) * fw_ref[...]
    o_ref[...] = out


def _out_proj(ya, yn, proj, x2, w_attn_out, w_ssm_out, w_o, final_w, final_norm):
    t = x2.shape[0]
    tm = min(OUT_TM, t)

    def const(shape):
        return pl.BlockSpec(shape, lambda i: (0, 0))

    return pl.pallas_call(
        functools.partial(_outproj_kernel, final_norm=final_norm),
        grid=(t // tm,),
        in_specs=[
            pl.BlockSpec((tm, SB_WIDTH), lambda i: (i, 0)),
            pl.BlockSpec((tm, SSD_WIDTH), lambda i: (i, 0)),
            pl.BlockSpec((tm, D_MODEL), lambda i: (i, GA_OFF // D_MODEL)),
            pl.BlockSpec((tm, D_MODEL), lambda i: (i, GS_OFF // D_MODEL)),
            pl.BlockSpec((tm, D_MODEL), lambda i: (i, 0)),
            const((SB_WIDTH, D_MODEL)),
            const((SSD_WIDTH, D_MODEL)),
            const((D_MODEL, D_MODEL)),
            const((1, D_MODEL)),
        ],
        out_specs=pl.BlockSpec((tm, D_MODEL), lambda i: (i, 0)),
        out_shape=jax.ShapeDtypeStruct((t, D_MODEL), F32),
        compiler_params=pltpu.CompilerParams(
            dimension_semantics=("parallel",), vmem_limit_bytes=VMEM_LIMIT),
        name="out_proj",
    )(ya, yn, proj, proj, x2, w_attn_out.astype(BF16), w_ssm_out.astype(BF16), w_o.astype(BF16),
      final_w.astype(F32).reshape(1, -1))


def _hybrid_layer(x2, batch, seq, norm_w, w_in, conv_w, conv_b, dt_bias, a_log, d_skip, ssm_norm_w,
                  w_attn_out, w_ssm_out, w_o, final_w, final_norm):
    q_scale = SB_HEAD_DIM ** -0.5 * math.log2(math.e)
    w_all_t = w_in.T.astype(BF16)
    w_gate_t = w_all_t[GATE_SRC_OFF:]
    w_dt_t = jnp.pad(w_all_t[DT_SRC_OFF:GATE_SRC_OFF], ((0, LANES - SSD_HEADS), (0, 0)))
    proj, dt_raw = _in_proj(x2, norm_w.astype(F32).reshape(1, -1), w_all_t, w_gate_t, w_dt_t, q_scale)
    ya = _sb_attention(proj, batch, seq)
    yn = _ssd(proj, dt_raw, conv_w, conv_b, dt_bias, a_log, d_skip, ssm_norm_w, batch, seq)
    return _out_proj(ya, yn, proj, x2, w_attn_out, w_ssm_out, w_o, final_w, final_norm)


def kernel(x, norm_w, w_in, conv_w, conv_b, dt_bias, a_log, d_skip, ssm_norm_w, w_attn_out, w_ssm_out,
           w_o, final_norm_w):
    batch, seq, _ = x.shape
    depth = norm_w.shape[0]
    h = x.reshape(batch * seq, D_MODEL)
    for layer in range(depth):
        h = _hybrid_layer(h, batch, seq, norm_w[layer], w_in[layer], conv_w[layer], conv_b[layer],
                          dt_bias[layer], a_log[layer], d_skip[layer], ssm_norm_w[layer],
                          w_attn_out[layer], w_ssm_out[layer], w_o[layer], final_norm_w,
                          final_norm=(layer == depth - 1))
    return h.reshape(batch, seq, D_MODEL)
```

```python
import functools
import math

import jax
import jax.numpy as jnp
from jax import lax
from jax.experimental import pallas as pl
from jax.experimental.pallas import tpu as pltpu

F32 = jnp.float32
BF16 = jnp.bfloat16

LANES = 128
D_MODEL = 1024
SB_HEADS = 16
SB_HEAD_DIM = 64
SB_WIDTH = SB_HEADS * SB_HEAD_DIM
SSD_WIDTH = 2 * D_MODEL
SSD_HEAD_DIM = 64
SSD_HEADS = SSD_WIDTH // SSD_HEAD_DIM
SSD_GROUPS = 4
SSD_GROUP_WIDTH = SSD_WIDTH // SSD_GROUPS
SSD_STATE = 128
SSD_CONV = 4
SSD_CHUNK = 128
SSD_STEP_CHUNKS = 4
SSD_CONV_STRIP = 256
SSD_TAIL = 16
SSD_BC_WIDTH = SSD_GROUPS * SSD_STATE
SSD_CONV_DIM = SSD_WIDTH + 2 * SSD_BC_WIDTH
EPS = 1e-6

Q_OFF = 0
K_OFF = Q_OFF + SB_WIDTH
V_OFF = K_OFF + SB_WIDTH
ZA_OFF = V_OFF + SB_WIDTH
ZS_OFF = ZA_OFF + SB_WIDTH
XBC_OFF = ZS_OFF + SSD_WIDTH
GA_OFF = XBC_OFF + SSD_CONV_DIM
GS_OFF = GA_OFF + D_MODEL
N_MAIN = GS_OFF + D_MODEL
DT_SRC_OFF = XBC_OFF + SSD_CONV_DIM
GATE_SRC_OFF = DT_SRC_OFF + SSD_HEADS

VMEM_LIMIT = 56 * 1024 * 1024

IN_TM = 2048
IN_TN = 1024
IN_HEAD_TILES = DT_SRC_OFF // IN_TN
ATTN_TQ = 64
ATTN_TK = 256
ATTN_PAIRS = 4
ATTN_SKIP_LOG2 = 160.0
ATTN_MASK_BIAS = -1e30
OUT_TM = 1024


def _dot(a, b):
    return jnp.dot(a, b, preferred_element_type=F32)


def _dot_nt(a, b):
    return lax.dot_general(a, b, (((1,), (1,)), ((), ())), preferred_element_type=F32)


def _split2(v):
    hi = v.astype(BF16)
    lo = (v - hi.astype(F32)).astype(BF16)
    return hi, lo


def _sigmoid(v):
    return 0.5 + 0.5 * jnp.tanh(0.5 * v)


def _silu(v):
    h = 0.5 * v
    return h + h * jnp.tanh(h)


def _inproj_kernel(x_ref, nw_ref, w_ref, wg_ref, wdt_ref, o_ref, dt_ref, h_ref, *, q_scale):
    j = pl.program_id(1)

    @pl.when(j == 0)
    def _():
        x = x_ref[...]
        ms = jnp.mean(x * x, axis=-1, keepdims=True)
        h = ((x * lax.rsqrt(ms + EPS)) * nw_ref[...]).astype(BF16)
        h_ref[...] = h
        dt_ref[...] = _dot_nt(h, wdt_ref[...])

    w = jnp.where(j < IN_HEAD_TILES, w_ref[...], wg_ref[...])
    scale = jnp.where(j == 0, q_scale, 1.0).astype(F32)
    o_ref[...] = (_dot_nt(h_ref[...], w) * scale).astype(o_ref.dtype)


def _in_proj(x2, norm_w, w_all_t, w_gate_t, w_dt_t, q_scale):
    t = x2.shape[0]
    tm = min(IN_TM, t)
    return pl.pallas_call(
        functools.partial(_inproj_kernel, q_scale=q_scale),
        grid=(t // tm, N_MAIN // IN_TN),
        in_specs=[
            pl.BlockSpec((tm, D_MODEL), lambda i, j: (i, 0)),
            pl.BlockSpec((1, D_MODEL), lambda i, j: (0, 0)),
            pl.BlockSpec((IN_TN, D_MODEL), lambda i, j: (jnp.minimum(j, IN_HEAD_TILES - 1), 0)),
            pl.BlockSpec((IN_TN, D_MODEL), lambda i, j: (jnp.maximum(j - IN_HEAD_TILES, 0), 0)),
            pl.BlockSpec((LANES, D_MODEL), lambda i, j: (0, 0)),
        ],
        out_specs=[
            pl.BlockSpec((tm, IN_TN), lambda i, j: (i, j)),
            pl.BlockSpec((tm, LANES), lambda i, j: (i, 0)),
        ],
        out_shape=[
            jax.ShapeDtypeStruct((t, N_MAIN), BF16),
            jax.ShapeDtypeStruct((t, LANES), F32),
        ],
        scratch_shapes=[pltpu.VMEM((tm, D_MODEL), BF16)],
        compiler_params=pltpu.CompilerParams(
            dimension_semantics=("parallel", "arbitrary"), vmem_limit_bytes=VMEM_LIMIT),
        name="in_proj",
    )(x2, norm_w, w_all_t, w_gate_t, w_dt_t)


def _attn_kernel(q_ref, k_ref, v_ref, za_ref, o_ref, kp_ref, vp_ref, bias_ref, sp0_ref, sp1_ref,
                 arg0_ref, arg1_ref, acc_ref, c_ref, *, seq, pairs):
    tq, tk = ATTN_TQ, ATTN_TK
    pad = tk - tq
    nqb = tk // tq
    width = pairs * LANES
    njj = seq // tk
    sp_refs = (sp0_ref, sp1_ref)
    arg_refs = (arg0_ref, arg1_ref)
    sign = jnp.uint32(0x80000000)
    lane = lax.broadcasted_iota(jnp.int32, (1, LANES), 1)
    first = lane < SB_HEAD_DIM

    kp_ref[0:pad, :] = jnp.zeros((pad, width), BF16)
    kp_ref[pad:pad + seq, :] = k_ref[...]
    vp_ref[0:pad, :] = jnp.zeros((pad, width), BF16)
    vp_ref[pad:pad + seq, :] = v_ref[...]

    @pl.when(jnp.logical_and(pl.program_id(0) == 0, pl.program_id(1) == 0))
    def _():
        acc_ref[0] = jnp.zeros(acc_ref.shape[1:], F32)
        c_ref[0] = jnp.zeros(c_ref.shape[1:], F32)


    row = lax.broadcasted_iota(jnp.int32, (tq, LANES), 0)
    col = lax.broadcasted_iota(jnp.int32, (tq, LANES), 1)
    bias_ref[0] = jnp.where(col < row + (LANES - tq), 0.0, ATTN_MASK_BIAS)
    bias_ref[1] = jnp.zeros((tq, LANES), F32)
    trow = lax.broadcasted_iota(jnp.int32, (tk, tk), 0)
    tcol = lax.broadcasted_iota(jnp.int32, (tk, tk), 1)
    tri = jnp.where(trow > tcol, 1.0, 0.0).astype(BF16)

    def offsets(jj, w, qb):
        qoff = pl.multiple_of((nqb * jj + qb) * tq, tq)
        koff = pl.multiple_of((jj - w) * tk + qb * tq, tq)
        return qoff, koff

    groups = [(qb, hp) for qb in range(nqb) for hp in range(pairs)]

    def cols_of(hp):
        return slice(hp * LANES, (hp + 1) * LANES)

    def scores(jj, w, qb, hp):
        qoff, koff = offsets(jj, w, qb)
        k = kp_ref[pl.ds(koff, tk), cols_of(hp)]
        bias = bias_ref[jnp.minimum(w, 1)]
        q = q_ref[pl.ds(qoff, tq), cols_of(hp)]
        zero = jnp.zeros_like(q)
        zs = [_dot_nt(qh, k) for qh in (jnp.where(first, q, zero), jnp.where(first, zero, q))]
        return [jnp.concatenate([z[:, :tk - LANES], z[:, tk - LANES:] + bias], axis=1) for z in zs]

    def carried(w, slot, qb, hp, zs):
        wsel = jnp.minimum(w, 1)
        cmin = None
        for hh in range(2):
            z = zs[hh]
            neg_abs = pltpu.bitcast(pltpu.bitcast(z, jnp.uint32) | sign, F32)
            sp = jnp.maximum(z, 0.0) + jnp.log2(1.0 + jnp.exp2(neg_abs))
            sp_refs[slot][qb, hp, hh] = sp.astype(BF16)
            c_in = c_ref[wsel, qb, hp, hh]
            arg_refs[slot][qb, hp, hh] = (z - sp) - jnp.concatenate([c_in] * (tk // LANES), axis=1)
            total = jnp.broadcast_to(jnp.sum(sp, axis=-1, keepdims=True), (tq, LANES))
            c_new = c_in + total
            c_ref[1, qb, hp, hh] = c_new
            cmin = c_new if cmin is None else jnp.minimum(cmin, c_new)
        return cmin

    def laters(slot, qb, hp):
        return [_dot(sp_refs[slot][qb, hp, hh], tri) for hh in range(2)]

    def weighted(jj, w, slot, qb, hp, lat):
        _, koff = offsets(jj, w, qb)
        v = vp_ref[pl.ds(koff, tk), cols_of(hp)]
        pv = [_dot(jnp.exp2(arg_refs[slot][qb, hp, hh] - lat[hh]).astype(BF16), v) for hh in range(2)]
        return jnp.where(first, pv[0], pv[1])

    def stage_a_all(jj, w, slot):
        cmin = None
        for qb, hp in groups:
            c = carried(w, slot, qb, hp, scores(jj, w, qb, hp))
            cmin = c if cmin is None else jnp.minimum(cmin, c)
        return jnp.min(cmin)

    def finalize(jj):
        for qb, hp in groups:
            qoff, _ = offsets(jj, 0, qb)
            za = za_ref[pl.ds(qoff, tq), cols_of(hp)].astype(F32)
            o_ref[pl.ds(qoff, tq), cols_of(hp)] = (acc_ref[1, qb, hp] * _silu(za)).astype(o_ref.dtype)

    def next_item(jj, w, cmin):
        more = jnp.logical_and(w < jj, cmin < ATTN_SKIP_LOG2)
        return jnp.where(more, jj, jj + 1), jnp.where(more, w + 1, 0)

    def tick(cj, cw, nj, nw, slot):
        aj = jnp.minimum(nj, njj - 1)
        csel = jnp.minimum(cw, 1)
        pending = (laters(slot, *groups[0]), scores(aj, nw, *groups[0]))
        cmin = None
        for g, (qb, hp) in enumerate(groups):
            lat, zs = pending
            if g + 1 < len(groups):
                pending = (laters(slot, *groups[g + 1]), scores(aj, nw, *groups[g + 1]))
            contrib = weighted(cj, cw, slot, qb, hp, lat)
            c = carried(nw, 1 - slot, qb, hp, zs)
            cmin = c if cmin is None else jnp.minimum(cmin, c)
            acc_ref[1, qb, hp] = acc_ref[csel, qb, hp] + contrib
        cmin = jnp.min(cmin)

        @pl.when(nj != cj)
        def _():
            finalize(cj)

        return next_item(nj, nw, cmin)

    zero = jnp.int32(0)
    nj0, nw0 = next_item(zero, zero, stage_a_all(zero, zero, 0))

    def cond(state):
        return state[0] < njj

    def body(state):
        cj, cw, nj, nw = state
        n2j, n2w = tick(cj, cw, nj, nw, 0)

        def second_tick():
            n3j, n3w = tick(nj, nw, n2j, n2w, 1)
            return n2j, n2w, n3j, n3w

        return lax.cond(nj < njj, second_tick, lambda: (nj, nw, n2j, n2w))

    lax.while_loop(cond, body, (zero, zero, nj0, nw0))


def _sb_attention(proj, batch, seq):
    pairs = ATTN_PAIRS
    width = pairs * LANES
    tq, tk = ATTN_TQ, ATTN_TK
    nqb = tk // tq
    assert seq % tk == 0 and tk % tq == 0 and tq <= LANES

    def col_spec(off):
        return pl.BlockSpec((seq, width), lambda b, p: (b, off // width + p))

    return pl.pallas_call(
        functools.partial(_attn_kernel, seq=seq, pairs=pairs),
        grid=(batch, SB_WIDTH // width),
        in_specs=[col_spec(Q_OFF), col_spec(K_OFF), col_spec(V_OFF), col_spec(ZA_OFF)],
        out_specs=pl.BlockSpec((seq, width), lambda b, p: (b, p)),
        out_shape=jax.ShapeDtypeStruct((batch * seq, SB_WIDTH), BF16),
        scratch_shapes=[
            pltpu.VMEM((tk - tq + seq, width), BF16),
            pltpu.VMEM((tk - tq + seq, width), BF16),
            pltpu.VMEM((2, tq, LANES), F32),
            pltpu.VMEM((nqb, pairs, 2, tq, tk), BF16),
            pltpu.VMEM((nqb, pairs, 2, tq, tk), BF16),
            pltpu.VMEM((nqb, pairs, 2, tq, tk), F32),
            pltpu.VMEM((nqb, pairs, 2, tq, tk), F32),
            pltpu.VMEM((2, nqb, pairs, tq, LANES), F32),
            pltpu.VMEM((2, nqb, pairs, 2, tq, LANES), F32),
        ],
        compiler_params=pltpu.CompilerParams(
            dimension_semantics=("arbitrary", "arbitrary"), vmem_limit_bytes=VMEM_LIMIT),
        name="sb_attn",
    )(proj, proj, proj, proj)


def _ssd_kernel(xbc_ref, zs_ref, dtr_ref, cw_ref, cb_ref, dtb_ref, alog_ref, dsk_ref, nw_ref,
                exp_ref, o_ref, xtail_ref, u_ref, h_ref, y_ref):
    @pl.when(pl.program_id(1) == 0)
    def _():
        xtail_ref[...] = jnp.zeros_like(xtail_ref)
        h_ref[...] = jnp.zeros_like(h_ref)

    for i in range(SSD_STEP_CHUNKS):
        _ssd_chunk(slice(i * SSD_CHUNK, (i + 1) * SSD_CHUNK), xbc_ref, zs_ref, dtr_ref, cw_ref, cb_ref,
                   dtb_ref, alog_ref, dsk_ref, nw_ref, exp_ref, o_ref, xtail_ref, u_ref, h_ref, y_ref)


def _ssd_chunk(rows, xbc_ref, zs_ref, dtr_ref, cw_ref, cb_ref, dtb_ref, alog_ref, dsk_ref, nw_ref,
               exp_ref, o_ref, xtail_ref, u_ref, h_ref, y_ref):
    L = SSD_CHUNK
    tail = SSD_TAIL

    srow = lax.broadcasted_iota(jnp.int32, (L, tail + L), 0)
    scol = lax.broadcasted_iota(jnp.int32, (L, tail + L), 1)
    picks = [jnp.where(scol == srow + (tail - (SSD_CONV - 1 - k)), 1.0, 0.0).astype(BF16)
             for k in range(SSD_CONV - 1)]
    for c0 in range(0, SSD_CONV_DIM, SSD_CONV_STRIP):
        cs = slice(c0, c0 + SSD_CONV_STRIP)
        x_b = xbc_ref[rows, cs]
        xcat = jnp.concatenate([xtail_ref[:, cs], x_b], axis=0)
        acc = cb_ref[:, cs] + cw_ref[SSD_CONV - 1:SSD_CONV, cs] * x_b.astype(F32)
        for k in range(SSD_CONV - 1):
            acc = acc + cw_ref[k:k + 1, cs] * _dot(picks[k], xcat)
        u_ref[:, cs] = _silu(acc)
    xtail_ref[...] = xbc_ref[rows.stop - tail:rows.stop, :]

    dtin = dtr_ref[rows, :] + dtb_ref[...]
    dt = jnp.maximum(dtin, 0.0) + jnp.log(1.0 + jnp.exp(-jnp.abs(dtin)))
    d_a = dt * (-jnp.exp(alog_ref[...]))
    row = lax.broadcasted_iota(jnp.int32, (L, L), 0)
    col = lax.broadcasted_iota(jnp.int32, (L, L), 1)
    causal = row >= col
    ltri = jnp.where(causal, 1.0, 0.0).astype(BF16)
    utri = jnp.where(row <= col, 1.0, 0.0).astype(BF16)
    a_cs = _dot(jnp.concatenate([ltri, ltri], axis=1),
                jnp.concatenate(_split2(d_a), axis=0))
    a_cs_t = _dot(jnp.concatenate(_split2(d_a.T), axis=1),
                  jnp.concatenate([utri, utri], axis=0))

    def stacked(v):
        return jnp.concatenate(_split2(v), axis=1)

    dt_hl = stacked(dt)
    ea_hl = stacked(jnp.exp(a_cs))
    dte_hl = stacked(jnp.exp(a_cs[L - 1:L, :] - a_cs))

    lane = lax.broadcasted_iota(jnp.int32, (1, LANES), 1)
    first = lane < SSD_HEAD_DIM
    heads_per_group = SSD_HEADS // SSD_GROUPS
    for g in range(SSD_GROUPS):
        gs = slice(g * SSD_GROUP_WIDTH, (g + 1) * SSD_GROUP_WIDTH)
        b_off = SSD_WIDTH + g * SSD_STATE
        c_off = SSD_WIDTH + SSD_BC_WIDTH + g * SSD_STATE
        expand = exp_ref[:, gs]
        xs = u_ref[:, gs]
        xdt = xs * _dot(dt_hl, expand)
        xdt_b = xdt.astype(BF16)
        xdte_b = (xdt * _dot(dte_hl, expand)).astype(BF16)
        ea_full = _dot(ea_hl, expand)
        bg = u_ref[:, b_off:b_off + SSD_STATE]
        cg_b = u_ref[:, c_off:c_off + SSD_STATE].astype(BF16)
        cb = _dot_nt(cg_b, bg.astype(BF16))
        states = _dot(bg.T.astype(BF16), xdte_b)
        h_in = h_ref[g]
        y_off = _dot(cg_b, h_in.astype(BF16)) * ea_full
        h_ref[g] = h_in * ea_full[L - 1:L, :] + states
        for j in range(heads_per_group // 2):
            pc = g * SSD_GROUP_WIDTH + j * LANES
            xpair = xdt_b[:, j * LANES:(j + 1) * LANES]
            zero = jnp.zeros_like(xpair)
            xh = (jnp.where(first, xpair, zero), jnp.where(first, zero, xpair))
            ypair = None
            for hh in range(2):
                h = g * heads_per_group + 2 * j + hh
                seg = a_cs[:, h:h + 1] - a_cs_t[h:h + 1, :]
                m = jnp.where(causal, cb * jnp.exp(seg), 0.0).astype(BF16)
                yh = _dot(m, xh[hh])
                ypair = yh if ypair is None else ypair + yh
            y_ref[:, pc:pc + LANES] = ypair
        y = y_ref[:, gs] + y_off + xs * dsk_ref[:, gs]
        y = y * _silu(zs_ref[rows, gs].astype(F32))
        ms = jnp.mean(y * y, axis=-1, keepdims=True)
        o_ref[rows, gs] = (y * lax.rsqrt(ms + EPS) * nw_ref[:, gs]).astype(o_ref.dtype)


def _ssd(proj, dt_raw, conv_w, conv_b, dt_bias, a_log, d_skip, ssm_norm_w, batch, seq):
    L = SSD_CHUNK * SSD_STEP_CHUNKS
    nc = seq // L
    expand = (jnp.arange(SSD_WIDTH)[None, :] // SSD_HEAD_DIM == jnp.arange(LANES)[:, None]).astype(BF16)
    expand = jnp.concatenate([expand, expand], axis=0)

    def pad_heads(v):
        return jnp.pad(v.astype(F32), (0, LANES - SSD_HEADS)).reshape(1, LANES)

    def const(shape):
        return pl.BlockSpec(shape, lambda b, c: (0, 0))

    return pl.pallas_call(
        _ssd_kernel,
        grid=(batch, nc),
        in_specs=[
            pl.BlockSpec((L, SSD_CONV_DIM), lambda b, c: (b * nc + c, XBC_OFF // SSD_CONV_DIM)),
            pl.BlockSpec((L, SSD_WIDTH), lambda b, c: (b * nc + c, ZS_OFF // SSD_WIDTH)),
            pl.BlockSpec((L, LANES), lambda b, c: (b * nc + c, 0)),
            const((SSD_CONV, SSD_CONV_DIM)),
            const((1, SSD_CONV_DIM)),
            const((1, LANES)),
            const((1, LANES)),
            const((1, SSD_WIDTH)),
            const((1, SSD_WIDTH)),
            const((2 * LANES, SSD_WIDTH)),
        ],
        out_specs=pl.BlockSpec((L, SSD_WIDTH), lambda b, c: (b * nc + c, 0)),
        out_shape=jax.ShapeDtypeStruct((batch * seq, SSD_WIDTH), BF16),
        scratch_shapes=[
            pltpu.VMEM((SSD_TAIL, SSD_CONV_DIM), BF16),
            pltpu.VMEM((SSD_CHUNK, SSD_CONV_DIM), F32),
            pltpu.VMEM((SSD_GROUPS, SSD_STATE, SSD_GROUP_WIDTH), F32),
            pltpu.VMEM((SSD_CHUNK, SSD_WIDTH), F32),
        ],
        compiler_params=pltpu.CompilerParams(
            dimension_semantics=("parallel", "arbitrary"), vmem_limit_bytes=VMEM_LIMIT),
        name="ssd",
    )(proj, proj, dt_raw, conv_w.astype(F32), conv_b.astype(F32).reshape(1, -1), pad_heads(dt_bias),
      pad_heads(a_log), jnp.repeat(d_skip.astype(F32), SSD_HEAD_DIM).reshape(1, -1),
      ssm_norm_w.astype(F32).reshape(1, -1), expand)


def _outproj_kernel(ya_ref, yn_ref, ga_ref, gs_ref, x_ref, wa_ref, ws_ref, wo_ref, fw_ref, o_ref,
                    *, final_norm):
    y_a = _dot(ya_ref[...], wa_ref[...])
    y_s = _dot(yn_ref[...], ws_ref[...])
    merged = _sigmoid(ga_ref[...].astype(F32)) * y_a + _sigmoid(gs_ref[...].astype(F32)) * y_s
    out = x_ref[...] + _dot(merged.astype(BF16), wo_ref[...])
    if final_norm:
        ms = jnp.mean(out * out, axis=-1, keepdims=True)
        out = (out * lax.rsqrt(ms + EPS)) * fw_ref[...]
    o_ref[...] = out


def _out_proj(ya, yn, proj, x2, w_attn_out, w_ssm_out, w_o, final_w, final_norm):
    t = x2.shape[0]
    tm = min(OUT_TM, t)

    def const(shape):
        return pl.BlockSpec(shape, lambda i: (0, 0))

    return pl.pallas_call(
        functools.partial(_outproj_kernel, final_norm=final_norm),
        grid=(t // tm,),
        in_specs=[
            pl.BlockSpec((tm, SB_WIDTH), lambda i: (i, 0)),
            pl.BlockSpec((tm, SSD_WIDTH), lambda i: (i, 0)),
            pl.BlockSpec((tm, D_MODEL), lambda i: (i, GA_OFF // D_MODEL)),
            pl.BlockSpec((tm, D_MODEL), lambda i: (i, GS_OFF // D_MODEL)),
            pl.BlockSpec((tm, D_MODEL), lambda i: (i, 0)),
            const((SB_WIDTH, D_MODEL)),
            const((SSD_WIDTH, D_MODEL)),
            const((D_MODEL, D_MODEL)),
            const((1, D_MODEL)),
        ],
        out_specs=pl.BlockSpec((tm, D_MODEL), lambda i: (i, 0)),
        out_shape=jax.ShapeDtypeStruct((t, D_MODEL), F32),
        compiler_params=pltpu.CompilerParams(
            dimension_semantics=("parallel",), vmem_limit_bytes=VMEM_LIMIT),
        name="out_proj",
    )(ya, yn, proj, proj, x2, w_attn_out.astype(BF16), w_ssm_out.astype(BF16), w_o.astype(BF16),
      final_w.astype(F32).reshape(1, -1))


def _hybrid_layer(x2, batch, seq, norm_w, w_in, conv_w, conv_b, dt_bias, a_log, d_skip, ssm_norm_w,
                  w_attn_out, w_ssm_out, w_o, final_w, final_norm):
    q_scale = SB_HEAD_DIM ** -0.5 * math.log2(math.e)
    w_all_t = w_in.T.astype(BF16)
    w_gate_t = w_all_t[GATE_SRC_OFF:]
    w_dt_t = jnp.pad(w_all_t[DT_SRC_OFF:GATE_SRC_OFF], ((0, LANES - SSD_HEADS), (0, 0)))
    proj, dt_raw = _in_proj(x2, norm_w.astype(F32).reshape(1, -1), w_all_t, w_gate_t, w_dt_t, q_scale)
    ya = _sb_attention(proj, batch, seq)
    yn = _ssd(proj, dt_raw, conv_w, conv_b, dt_bias, a_log, d_skip, ssm_norm_w, batch, seq)
    return _out_proj(ya, yn, proj, x2, w_attn_out, w_ssm_out, w_o, final_w, final_norm)


def kernel(x, norm_w, w_in, conv_w, conv_b, dt_bias, a_log, d_skip, ssm_norm_w, w_attn_out, w_ssm_out,
           w_o, final_norm_w):
    batch, seq, _ = x.shape
    depth = norm_w.shape[0]
    h = x.reshape(batch * seq, D_MODEL)
    for layer in range(depth):
        h = _hybrid_layer(h, batch, seq, norm_w[layer], w_in[layer], conv_w[layer], conv_b[layer],
                          dt_bias[layer], a_log[layer], d_skip[layer], ssm_norm_w[layer],
                          w_attn_out[layer], w_ssm_out[layer], w_o[layer], final_norm_w,
                          final_norm=(layer == depth - 1))
    return h.reshape(batch, seq, D_MODEL)
```

```python
import functools
import math

import jax
import jax.numpy as jnp
from jax import lax
from jax.experimental import pallas as pl
from jax.experimental.pallas import tpu as pltpu

F32 = jnp.float32
BF16 = jnp.bfloat16

LANES = 128
D_MODEL = 1024
SB_HEADS = 16
SB_HEAD_DIM = 64
SB_WIDTH = SB_HEADS * SB_HEAD_DIM
SSD_WIDTH = 2 * D_MODEL
SSD_HEAD_DIM = 64
SSD_HEADS = SSD_WIDTH // SSD_HEAD_DIM
SSD_GROUPS = 4
SSD_GROUP_WIDTH = SSD_WIDTH // SSD_GROUPS
SSD_STATE = 128
SSD_CONV = 4
SSD_CHUNK = 128
SSD_STEP_CHUNKS = 4
SSD_CONV_STRIP = 256
SSD_TAIL = 16
SSD_BC_WIDTH = SSD_GROUPS * SSD_STATE
SSD_CONV_DIM = SSD_WIDTH + 2 * SSD_BC_WIDTH
EPS = 1e-6

Q_OFF = 0
K_OFF = Q_OFF + SB_WIDTH
V_OFF = K_OFF + SB_WIDTH
ZA_OFF = V_OFF + SB_WIDTH
ZS_OFF = ZA_OFF + SB_WIDTH
XBC_OFF = ZS_OFF + SSD_WIDTH
GA_OFF = XBC_OFF + SSD_CONV_DIM
GS_OFF = GA_OFF + D_MODEL
N_MAIN = GS_OFF + D_MODEL
DT_SRC_OFF = XBC_OFF + SSD_CONV_DIM
GATE_SRC_OFF = DT_SRC_OFF + SSD_HEADS

VMEM_LIMIT = 48 * 1024 * 1024

IN_TM = 2048
IN_TN = 1024
IN_HEAD_TILES = DT_SRC_OFF // IN_TN
ATTN_TQ = 64
ATTN_TK = 256
ATTN_PAIRS = 4
ATTN_SKIP_LOG2 = 160.0
ATTN_MASK_BIAS = -1e30
OUT_TM = 512


def _dot(a, b):
    return jnp.dot(a, b, preferred_element_type=F32)


def _dot_nt(a, b):
    return lax.dot_general(a, b, (((1,), (1,)), ((), ())), preferred_element_type=F32)


def _split2(v):
    hi = v.astype(BF16)
    lo = (v - hi.astype(F32)).astype(BF16)
    return hi, lo


def _sigmoid(v):
    return 0.5 + 0.5 * jnp.tanh(0.5 * v)


def _silu(v):
    h = 0.5 * v
    return h + h * jnp.tanh(h)


def _inproj_kernel(x_ref, nw_ref, w_ref, wg_ref, wdt_ref, o_ref, dt_ref, h_ref, *, q_scale):
    j = pl.program_id(1)

    @pl.when(j == 0)
    def _():
        x = x_ref[...]
        ms = jnp.mean(x * x, axis=-1, keepdims=True)
        h = ((x * lax.rsqrt(ms + EPS)) * nw_ref[...]).astype(BF16)
        h_ref[...] = h
        dt_ref[...] = _dot_nt(h, wdt_ref[...])

    w = jnp.where(j < IN_HEAD_TILES, w_ref[...], wg_ref[...])
    scale = jnp.where(j == 0, q_scale, 1.0).astype(F32)
    o_ref[...] = (_dot_nt(h_ref[...], w) * scale).astype(o_ref.dtype)


def _in_proj(x2, norm_w, w_all_t, w_gate_t, w_dt_t, q_scale):
    t = x2.shape[0]
    tm = min(IN_TM, t)
    return pl.pallas_call(
        functools.partial(_inproj_kernel, q_scale=q_scale),
        grid=(t // tm, N_MAIN // IN_TN),
        in_specs=[
            pl.BlockSpec((tm, D_MODEL), lambda i, j: (i, 0)),
            pl.BlockSpec((1, D_MODEL), lambda i, j: (0, 0)),
            pl.BlockSpec((IN_TN, D_MODEL), lambda i, j: (jnp.minimum(j, IN_HEAD_TILES - 1), 0)),
            pl.BlockSpec((IN_TN, D_MODEL), lambda i, j: (jnp.maximum(j - IN_HEAD_TILES, 0), 0)),
            pl.BlockSpec((LANES, D_MODEL), lambda i, j: (0, 0)),
        ],
        out_specs=[
            pl.BlockSpec((tm, IN_TN), lambda i, j: (i, j)),
            pl.BlockSpec((tm, LANES), lambda i, j: (i, 0)),
        ],
        out_shape=[
            jax.ShapeDtypeStruct((t, N_MAIN), BF16),
            jax.ShapeDtypeStruct((t, LANES), F32),
        ],
        scratch_shapes=[pltpu.VMEM((tm, D_MODEL), BF16)],
        compiler_params=pltpu.CompilerParams(
            dimension_semantics=("parallel", "arbitrary"), vmem_limit_bytes=VMEM_LIMIT),
        name="in_proj",
    )(x2, norm_w, w_all_t, w_gate_t, w_dt_t)


def _attn_kernel(q_ref, k_ref, v_ref, za_ref, o_ref, kp_ref, vp_ref, bias_ref, sp0_ref, sp1_ref,
                 arg0_ref, arg1_ref, acc_ref, c_ref, *, seq, pairs):
    tq, tk = ATTN_TQ, ATTN_TK
    pad = tk - tq
    nqb = tk // tq
    width = pairs * LANES
    njj = seq // tk
    sp_refs = (sp0_ref, sp1_ref)
    arg_refs = (arg0_ref, arg1_ref)
    sign = jnp.uint32(0x80000000)
    lane = lax.broadcasted_iota(jnp.int32, (1, LANES), 1)
    first = lane < SB_HEAD_DIM

    kp_ref[0:pad, :] = jnp.zeros((pad, width), BF16)
    kp_ref[pad:pad + seq, :] = k_ref[...]
    vp_ref[0:pad, :] = jnp.zeros((pad, width), BF16)
    vp_ref[pad:pad + seq, :] = v_ref[...]

    @pl.when(jnp.logical_and(pl.program_id(0) == 0, pl.program_id(1) == 0))
    def _():
        acc_ref[0] = jnp.zeros(acc_ref.shape[1:], F32)
        c_ref[0] = jnp.zeros(c_ref.shape[1:], F32)


    row = lax.broadcasted_iota(jnp.int32, (tq, LANES), 0)
    col = lax.broadcasted_iota(jnp.int32, (tq, LANES), 1)
    bias_ref[0] = jnp.where(col < row + (LANES - tq), 0.0, ATTN_MASK_BIAS)
    bias_ref[1] = jnp.zeros((tq, LANES), F32)
    trow = lax.broadcasted_iota(jnp.int32, (tk, tk), 0)
    tcol = lax.broadcasted_iota(jnp.int32, (tk, tk), 1)
    tri = jnp.where(trow > tcol, 1.0, 0.0).astype(BF16)

    def offsets(jj, w, qb):
        qoff = pl.multiple_of((nqb * jj + qb) * tq, tq)
        koff = pl.multiple_of((jj - w) * tk + qb * tq, tq)
        return qoff, koff

    groups = [(qb, hp) for qb in range(nqb) for hp in range(pairs)]

    def cols_of(hp):
        return slice(hp * LANES, (hp + 1) * LANES)

    def scores(jj, w, qb, hp):
        qoff, koff = offsets(jj, w, qb)
        k = kp_ref[pl.ds(koff, tk), cols_of(hp)]
        bias = bias_ref[jnp.minimum(w, 1)]
        q = q_ref[pl.ds(qoff, tq), cols_of(hp)]
        zero = jnp.zeros_like(q)
        zs = [_dot_nt(qh, k) for qh in (jnp.where(first, q, zero), jnp.where(first, zero, q))]
        return [jnp.concatenate([z[:, :tk - LANES], z[:, tk - LANES:] + bias], axis=1) for z in zs]

    def carried(w, slot, qb, hp, zs):
        wsel = jnp.minimum(w, 1)
        cmin = None
        for hh in range(2):
            z = zs[hh]
            neg_abs = pltpu.bitcast(pltpu.bitcast(z, jnp.uint32) | sign, F32)
            sp = jnp.maximum(z, 0.0) + jnp.log2(1.0 + jnp.exp2(neg_abs))
            sp_refs[slot][qb, hp, hh] = sp.astype(BF16)
            c_in = c_ref[wsel, qb, hp, hh]
            arg_refs[slot][qb, hp, hh] = (z - sp) - jnp.concatenate([c_in] * (tk // LANES), axis=1)
            total = jnp.broadcast_to(jnp.sum(sp, axis=-1, keepdims=True), (tq, LANES))
            c_new = c_in + total
            c_ref[1, qb, hp, hh] = c_new
            cmin = c_new if cmin is None else jnp.minimum(cmin, c_new)
        return cmin

    def laters(slot, qb, hp):
        return [_dot(sp_refs[slot][qb, hp, hh], tri) for hh in range(2)]

    def weighted(jj, w, slot, qb, hp, lat):
        _, koff = offsets(jj, w, qb)
        v = vp_ref[pl.ds(koff, tk), cols_of(hp)]
        pv = [_dot(jnp.exp2(arg_refs[slot][qb, hp, hh] - lat[hh]).astype(BF16), v) for hh in range(2)]
        return jnp.where(first, pv[0], pv[1])

    def stage_a_all(jj, w, slot):
        cmin = None
        for qb, hp in groups:
            c = carried(w, slot, qb, hp, scores(jj, w, qb, hp))
            cmin = c if cmin is None else jnp.minimum(cmin, c)
        return jnp.min(cmin)

    def finalize(jj):
        for qb, hp in groups:
            qoff, _ = offsets(jj, 0, qb)
            za = za_ref[pl.ds(qoff, tq), cols_of(hp)].astype(F32)
            o_ref[pl.ds(qoff, tq), cols_of(hp)] = (acc_ref[1, qb, hp] * _silu(za)).astype(o_ref.dtype)

    def next_item(jj, w, cmin):
        more = jnp.logical_and(w < jj, cmin < ATTN_SKIP_LOG2)
        return jnp.where(more, jj, jj + 1), jnp.where(more, w + 1, 0)

    def tick(cj, cw, nj, nw, slot):
        aj = jnp.minimum(nj, njj - 1)
        csel = jnp.minimum(cw, 1)
        pending = (laters(slot, *groups[0]), scores(aj, nw, *groups[0]))
        cmin = None
        for g, (qb, hp) in enumerate(groups):
            lat, zs = pending
            if g + 1 < len(groups):
                pending = (laters(slot, *groups[g + 1]), scores(aj, nw, *groups[g + 1]))
            contrib = weighted(cj, cw, slot, qb, hp, lat)
            c = carried(nw, 1 - slot, qb, hp, zs)
            cmin = c if cmin is None else jnp.minimum(cmin, c)
            acc_ref[1, qb, hp] = acc_ref[csel, qb, hp] + contrib
        cmin = jnp.min(cmin)

        @pl.when(nj != cj)
        def _():
            finalize(cj)

        return next_item(nj, nw, cmin)

    zero = jnp.int32(0)
    nj0, nw0 = next_item(zero, zero, stage_a_all(zero, zero, 0))

    def cond(state):
        return state[0] < njj

    def body(state):
        cj, cw, nj, nw = state
        n2j, n2w = tick(cj, cw, nj, nw, 0)

        def second_tick():
            n3j, n3w = tick(nj, nw, n2j, n2w, 1)
            return n2j, n2w, n3j, n3w

        return lax.cond(nj < njj, second_tick, lambda: (nj, nw, n2j, n2w))

    lax.while_loop(cond, body, (zero, zero, nj0, nw0))


def _sb_attention(proj, batch, seq):
    pairs = ATTN_PAIRS
    width = pairs * LANES
    tq, tk = ATTN_TQ, ATTN_TK
    nqb = tk // tq
    assert seq % tk == 0 and tk % tq == 0 and tq <= LANES

    def col_spec(off):
        return pl.BlockSpec((seq, width), lambda b, p: (b, off // width + p))

    return pl.pallas_call(
        functools.partial(_attn_kernel, seq=seq, pairs=pairs),
        grid=(batch, SB_WIDTH // width),
        in_specs=[col_spec(Q_OFF), col_spec(K_OFF), col_spec(V_OFF), col_spec(ZA_OFF)],
        out_specs=pl.BlockSpec((seq, width), lambda b, p: (b, p)),
        out_shape=jax.ShapeDtypeStruct((batch * seq, SB_WIDTH), BF16),
        scratch_shapes=[
            pltpu.VMEM((tk - tq + seq, width), BF16),
            pltpu.VMEM((tk - tq + seq, width), BF16),
            pltpu.VMEM((2, tq, LANES), F32),
            pltpu.VMEM((nqb, pairs, 2, tq, tk), BF16),
            pltpu.VMEM((nqb, pairs, 2, tq, tk), BF16),
            pltpu.VMEM((nqb, pairs, 2, tq, tk), F32),
            pltpu.VMEM((nqb, pairs, 2, tq, tk), F32),
            pltpu.VMEM((2, nqb, pairs, tq, LANES), F32),
            pltpu.VMEM((2, nqb, pairs, 2, tq, LANES), F32),
        ],
        compiler_params=pltpu.CompilerParams(
            dimension_semantics=("arbitrary", "arbitrary"), vmem_limit_bytes=VMEM_LIMIT),
        name="sb_attn",
    )(proj, proj, proj, proj)


def _ssd_kernel(xbc_ref, zs_ref, dtr_ref, cw_ref, cb_ref, dtb_ref, alog_ref, dsk_ref, nw_ref,
                exp_ref, o_ref, xtail_ref, u_ref, h_ref, y_ref):
    @pl.when(pl.program_id(1) == 0)
    def _():
        xtail_ref[...] = jnp.zeros_like(xtail_ref)
        h_ref[...] = jnp.zeros_like(h_ref)

    for i in range(SSD_STEP_CHUNKS):
        _ssd_chunk(slice(i * SSD_CHUNK, (i + 1) * SSD_CHUNK), xbc_ref, zs_ref, dtr_ref, cw_ref, cb_ref,
                   dtb_ref, alog_ref, dsk_ref, nw_ref, exp_ref, o_ref, xtail_ref, u_ref, h_ref, y_ref)


def _ssd_chunk(rows, xbc_ref, zs_ref, dtr_ref, cw_ref, cb_ref, dtb_ref, alog_ref, dsk_ref, nw_ref,
               exp_ref, o_ref, xtail_ref, u_ref, h_ref, y_ref):
    L = SSD_CHUNK
    tail = SSD_TAIL

    srow = lax.broadcasted_iota(jnp.int32, (L, tail + L), 0)
    scol = lax.broadcasted_iota(jnp.int32, (L, tail + L), 1)
    picks = [jnp.where(scol == srow + (tail - (SSD_CONV - 1 - k)), 1.0, 0.0).astype(BF16)
             for k in range(SSD_CONV - 1)]
    for c0 in range(0, SSD_CONV_DIM, SSD_CONV_STRIP):
        cs = slice(c0, c0 + SSD_CONV_STRIP)
        x_b = xbc_ref[rows, cs]
        xcat = jnp.concatenate([xtail_ref[:, cs], x_b], axis=0)
        acc = cb_ref[:, cs] + cw_ref[SSD_CONV - 1:SSD_CONV, cs] * x_b.astype(F32)
        for k in range(SSD_CONV - 1):
            acc = acc + cw_ref[k:k + 1, cs] * _dot(picks[k], xcat)
        u_ref[:, cs] = _silu(acc)
    xtail_ref[...] = xbc_ref[rows.stop - tail:rows.stop, :]

    dtin = dtr_ref[rows, :] + dtb_ref[...]
    dt = jnp.maximum(dtin, 0.0) + jnp.log(1.0 + jnp.exp(-jnp.abs(dtin)))
    d_a = dt * (-jnp.exp(alog_ref[...]))
    row = lax.broadcasted_iota(jnp.int32, (L, L), 0)
    col = lax.broadcasted_iota(jnp.int32, (L, L), 1)
    causal = row >= col
    ltri = jnp.where(causal, 1.0, 0.0).astype(BF16)
    utri = jnp.where(row <= col, 1.0, 0.0).astype(BF16)
    a_cs = _dot(jnp.concatenate([ltri, ltri], axis=1),
                jnp.concatenate(_split2(d_a), axis=0))
    a_cs_t = _dot(jnp.concatenate(_split2(d_a.T), axis=1),
                  jnp.concatenate([utri, utri], axis=0))

    def stacked(v):
        return jnp.concatenate(_split2(v), axis=1)

    dt_hl = stacked(dt)
    ea_hl = stacked(jnp.exp(a_cs))
    dte_hl = stacked(jnp.exp(a_cs[L - 1:L, :] - a_cs))

    lane = lax.broadcasted_iota(jnp.int32, (1, LANES), 1)
    first = lane < SSD_HEAD_DIM
    heads_per_group = SSD_HEADS // SSD_GROUPS
    half = SSD_GROUP_WIDTH // 2
    for g in range(SSD_GROUPS):
        gs = slice(g * SSD_GROUP_WIDTH, (g + 1) * SSD_GROUP_WIDTH)
        b_off = SSD_WIDTH + g * SSD_STATE
        c_off = SSD_WIDTH + SSD_BC_WIDTH + g * SSD_STATE
        bg = u_ref[:, b_off:b_off + SSD_STATE]
        bg_t = bg.T.astype(BF16)
        cg_b = u_ref[:, c_off:c_off + SSD_STATE].astype(BF16)
        cb = _dot_nt(cg_b, bg.astype(BF16))
        ssq = None
        for hf in range(2):
            hs = slice(g * SSD_GROUP_WIDTH + hf * half, g * SSD_GROUP_WIDTH + (hf + 1) * half)
            gl = slice(hf * half, (hf + 1) * half)
            expand = exp_ref[:, hs]
            xs = u_ref[:, hs]
            xdt = xs * _dot(dt_hl, expand)
            xdt_b = xdt.astype(BF16)
            xdte_b = (xdt * _dot(dte_hl, expand)).astype(BF16)
            ea_full = _dot(ea_hl, expand)
            states = _dot(bg_t, xdte_b)
            h_in = h_ref[g, :, gl]
            y = _dot(cg_b, h_in.astype(BF16)) * ea_full + xs * dsk_ref[:, hs]
            h_ref[g, :, gl] = h_in * ea_full[L - 1:L, :] + states
            ypairs = []
            for j in range(half // LANES):
                xpair = xdt_b[:, j * LANES:(j + 1) * LANES]
                zero = jnp.zeros_like(xpair)
                xh = (jnp.where(first, xpair, zero), jnp.where(first, zero, xpair))
                ypair = None
                for hh in range(2):
                    h = g * heads_per_group + hf * (heads_per_group // 2) + 2 * j + hh
                    seg = a_cs[:, h:h + 1] - a_cs_t[h:h + 1, :]
                    m = jnp.where(causal, cb * jnp.exp(seg), 0.0).astype(BF16)
                    yh = _dot(m, xh[hh])
                    ypair = yh if ypair is None else ypair + yh
                ypairs.append(ypair)
            y = (y + jnp.concatenate(ypairs, axis=1)) * _silu(zs_ref[rows, hs].astype(F32))
            y_ref[:, hs] = y
            part = jnp.sum(y * y, axis=-1, keepdims=True)
            ssq = part if ssq is None else ssq + part
        scale = lax.rsqrt(ssq * (1.0 / SSD_GROUP_WIDTH) + EPS)
        o_ref[rows, gs] = (y_ref[:, gs] * scale * nw_ref[:, gs]).astype(o_ref.dtype)


def _ssd(proj, dt_raw, conv_w, conv_b, dt_bias, a_log, d_skip, ssm_norm_w, batch, seq):
    L = SSD_CHUNK * SSD_STEP_CHUNKS
    nc = seq // L
    expand = (jnp.arange(SSD_WIDTH)[None, :] // SSD_HEAD_DIM == jnp.arange(LANES)[:, None]).astype(BF16)
    expand = jnp.concatenate([expand, expand], axis=0)

    def pad_heads(v):
        return jnp.pad(v.astype(F32), (0, LANES - SSD_HEADS)).reshape(1, LANES)

    def const(shape):
        return pl.BlockSpec(shape, lambda b, c: (0, 0))

    return pl.pallas_call(
        _ssd_kernel,
        grid=(batch, nc),
        in_specs=[
            pl.BlockSpec((L, SSD_CONV_DIM), lambda b, c: (b * nc + c, XBC_OFF // SSD_CONV_DIM)),
            pl.BlockSpec((L, SSD_WIDTH), lambda b, c: (b * nc + c, ZS_OFF // SSD_WIDTH)),
            pl.BlockSpec((L, LANES), lambda b, c: (b * nc + c, 0)),
            const((SSD_CONV, SSD_CONV_DIM)),
            const((1, SSD_CONV_DIM)),
            const((1, LANES)),
            const((1, LANES)),
            const((1, SSD_WIDTH)),
            const((1, SSD_WIDTH)),
            const((2 * LANES, SSD_WIDTH)),
        ],
        out_specs=pl.BlockSpec((L, SSD_WIDTH), lambda b, c: (b * nc + c, 0)),
        out_shape=jax.ShapeDtypeStruct((batch * seq, SSD_WIDTH), BF16),
        scratch_shapes=[
            pltpu.VMEM((SSD_TAIL, SSD_CONV_DIM), BF16),
            pltpu.VMEM((SSD_CHUNK, SSD_CONV_DIM), F32),
            pltpu.VMEM((SSD_GROUPS, SSD_STATE, SSD_GROUP_WIDTH), F32),
            pltpu.VMEM((SSD_CHUNK, SSD_WIDTH), F32),
        ],
        compiler_params=pltpu.CompilerParams(
            dimension_semantics=("parallel", "arbitrary"), vmem_limit_bytes=VMEM_LIMIT),
        name="ssd",
    )(proj, proj, dt_raw, conv_w.astype(F32), conv_b.astype(F32).reshape(1, -1), pad_heads(dt_bias),
      pad_heads(a_log), jnp.repeat(d_skip.astype(F32), SSD_HEAD_DIM).reshape(1, -1),
      ssm_norm_w.astype(F32).reshape(1, -1), expand)


def _outproj_kernel(ya_ref, yn_ref, ga_ref, gs_ref, x_ref, wa_ref, ws_ref, wo_ref, fw_ref, o_ref,
                    *, final_norm):
    y_a = _dot(ya_ref[...], wa_ref[...])
    y_s = _dot(yn_ref[...], ws_ref[...])
    merged = _sigmoid(ga_ref[...].astype(F32)) * y_a + _sigmoid(gs_ref[...].astype(F32)) * y_s
    out = x_ref[...] + _dot(merged.astype(BF16), wo_ref[...])
    if final_norm:
        ms = jnp.mean(out * out, axis=-1, keepdims=True)
        out = (out * lax.rsqrt(ms + EPS)) * fw_ref[...]
    o_ref[...] = out


def _out_proj(ya, yn, proj, x2, w_attn_out, w_ssm_out, w_o, final_w, final_norm):
    t = x2.shape[0]
    tm = min(OUT_TM, t)

    def const(shape):
        return pl.BlockSpec(shape, lambda i: (0, 0))

    return pl.pallas_call(
        functools.partial(_outproj_kernel, final_norm=final_norm),
        grid=(t // tm,),
        in_specs=[
            pl.BlockSpec((tm, SB_WIDTH), lambda i: (i, 0)),
            pl.BlockSpec((tm, SSD_WIDTH), lambda i: (i, 0)),
            pl.BlockSpec((tm, D_MODEL), lambda i: (i, GA_OFF // D_MODEL)),
            pl.BlockSpec((tm, D_MODEL), lambda i: (i, GS_OFF // D_MODEL)),
            pl.BlockSpec((tm, D_MODEL), lambda i: (i, 0)),
            const((SB_WIDTH, D_MODEL)),
            const((SSD_WIDTH, D_MODEL)),
            const((D_MODEL, D_MODEL)),
            const((1, D_MODEL)),
        ],
        out_specs=pl.BlockSpec((tm, D_MODEL), lambda i: (i, 0)),
        out_shape=jax.ShapeDtypeStruct((t, D_MODEL), F32),
        compiler_params=pltpu.CompilerParams(
            dimension_semantics=("parallel",), vmem_limit_bytes=VMEM_LIMIT),
        name="out_proj",
    )(ya, yn, proj, proj, x2, w_attn_out.astype(BF16), w_ssm_out.astype(BF16), w_o.astype(BF16),
      final_w.astype(F32).reshape(1, -1))


def _hybrid_layer(x2, batch, seq, norm_w, w_in, conv_w, conv_b, dt_bias, a_log, d_skip, ssm_norm_w,
                  w_attn_out, w_ssm_out, w_o, final_w, final_norm):
    q_scale = SB_HEAD_DIM ** -0.5 * math.log2(math.e)
    w_all_t = w_in.T.astype(BF16)
    w_gate_t = w_all_t[GATE_SRC_OFF:]
    w_dt_t = jnp.pad(w_all_t[DT_SRC_OFF:GATE_SRC_OFF], ((0, LANES - SSD_HEADS), (0, 0)))
    proj, dt_raw = _in_proj(x2, norm_w.astype(F32).reshape(1, -1), w_all_t, w_gate_t, w_dt_t, q_scale)
    ya = _sb_attention(proj, batch, seq)
    yn = _ssd(proj, dt_raw, conv_w, conv_b, dt_bias, a_log, d_skip, ssm_norm_w, batch, seq)
    return _out_proj(ya, yn, proj, x2, w_attn_out, w_ssm_out, w_o, final_w, final_norm)


def kernel(x, norm_w, w_in, conv_w, conv_b, dt_bias, a_log, d_skip, ssm_norm_w, w_attn_out, w_ssm_out,
           w_o, final_norm_w):
    batch, seq, _ = x.shape
    depth = norm_w.shape[0]
    h = x.reshape(batch * seq, D_MODEL)
    for layer in range(depth):
        h = _hybrid_layer(h, batch, seq, norm_w[layer], w_in[layer], conv_w[layer], conv_b[layer],
                          dt_bias[layer], a_log[layer], d_skip[layer], ssm_norm_w[layer],
                          w_attn_out[layer], w_ssm_out[layer], w_o[layer], final_norm_w,
                          final_norm=(layer == depth - 1))
    return h.reshape(batch, seq, D_MODEL)
```
